```python
import math
import jax, jax.numpy as jnp
from jax import lax
import numpy as np

D_MODEL = 1024
BATCH = 2
SEQ = 8192
DEPTH = 4

N_MIXERS = 2
N_GDN_LAYERS = (DEPTH + N_MIXERS - 1) // N_MIXERS
N_GLA_LAYERS = DEPTH // N_MIXERS
CHUNK = 64
CONV_K = 4

GDN_HEADS = 8
GDN_DK = 128
GDN_DV = 256
GDN_KEY = GDN_HEADS * GDN_DK
GDN_VAL = GDN_HEADS * GDN_DV
GDN_QKV = 2 * GDN_KEY + GDN_VAL
GDN_IN = GDN_QKV + GDN_VAL + 2 * GDN_HEADS

GLA_HEADS = 4
GLA_DK = 128
GLA_DV = 256
GLA_KEY = GLA_HEADS * GLA_DK
GLA_VAL = GLA_HEADS * GLA_DV
GLA_RANK = 16
GLA_GATE_NORM = 16.0
GLA_IN = 2 * GLA_KEY + 2 * GLA_VAL + GLA_RANK

DEEP_ALPHA = (2.0 * DEPTH) ** 0.25
DEEP_BETA = (8.0 * DEPTH) ** -0.25
LN_EPS = 1e-5
RMS_EPS = 1e-6
L2_EPS = 1e-6

kernel_name = "hybrid_gdn_gla_deepnorm"


def layer_norm(x, g, b):
    xf = x.astype(jnp.float32)
    mu = jnp.mean(xf, -1, keepdims=True)
    xc = xf - mu
    var = jnp.mean(xc * xc, -1, keepdims=True)
    y = xc * lax.rsqrt(var + LN_EPS) * g.astype(jnp.float32) + b.astype(jnp.float32)
    return y.astype(x.dtype)


def gated_rmsnorm(o, w, gate):
    o = o * lax.rsqrt(jnp.mean(o * o, -1, keepdims=True) + RMS_EPS)
    return o * w.astype(jnp.float32) * jax.nn.silu(gate.astype(jnp.float32))


def l2norm(t):
    return t * lax.rsqrt(jnp.sum(t * t, -1, keepdims=True) + L2_EPS)


def causal_conv_silu(x, w):
    y = lax.conv_general_dilated(
        x, w[:, None, :].astype(x.dtype), window_strides=(1,), padding=[(CONV_K - 1, 0)],
        dimension_numbers=('NWC', 'WIO', 'NWC'), feature_group_count=x.shape[-1])
    return jax.nn.silu(y)


def to_chunks(t):
    B, T, H, d = t.shape
    return t.reshape(B, T // CHUNK, CHUNK, H, d).transpose(0, 3, 1, 2, 4)


def from_scan_chunks(o):
    N, B, H, C, d = o.shape
    return o.transpose(1, 0, 3, 2, 4).reshape(B, N * C, H, d)


def gated_delta_rule(q, k, v, g, beta):
    B, T, H, dk = q.shape
    dv = v.shape[-1]
    q = to_chunks(q) * (dk ** -0.5)
    k = to_chunks(k)
    v = to_chunks(v)
    g = to_chunks(g[..., None])[..., 0]
    beta = to_chunks(beta[..., None])
    gc = jnp.cumsum(g, axis=-1)
    causal = jnp.tril(jnp.ones((CHUNK, CHUNK), bool))
    strict = jnp.tril(jnp.ones((CHUNK, CHUNK), bool), -1)
    diff = gc[..., :, None] - gc[..., None, :]
    decay = jnp.where(causal, jnp.exp(jnp.where(causal, diff, 0.0)), 0.0)
    kb = k * beta
    A = jnp.where(strict, jnp.einsum('bhncd,bhnsd->bhncs', kb, k) * decay, 0.0)
    eye = jnp.eye(CHUNK, dtype=A.dtype)
    rhs = jnp.concatenate([v * beta, kb * jnp.exp(gc)[..., None]], axis=-1)
    uw = lax.linalg.triangular_solve(A + eye, rhs, left_side=True, lower=True, unit_diagonal=True)
    u, w = uw[..., :dv], uw[..., dv:]
    qk = jnp.einsum('bhncd,bhnsd->bhncs', q, k) * decay
    q_dec = q * jnp.exp(gc)[..., None]
    k_dec = k * jnp.exp(gc[..., -1:] - gc)[..., None]
    g_last = jnp.exp(gc[..., -1])

    def step(S, inp):
        qk_n, qd_n, w_n, u_n, kd_n, gl_n = inp
        v_new = u_n - jnp.einsum('bhcd,bhde->bhce', w_n, S)
        o = jnp.einsum('bhcd,bhde->bhce', qd_n, S) + jnp.einsum('bhcs,bhse->bhce', qk_n, v_new)
        S = S * gl_n[..., None, None] + jnp.einsum('bhcd,bhce->bhde', kd_n, v_new)
        return S, o

    xs = tuple(jnp.moveaxis(a, 2, 0) for a in (qk, q_dec, w, u, k_dec, g_last))
    S0 = jnp.zeros((B, H, dk, dv), jnp.float32)
    _, o = lax.scan(step, S0, xs)
    return from_scan_chunks(o)


def gla_chunked(q, k, v, gk):
    B, T, H, dk = q.shape
    dv = v.shape[-1]
    q = to_chunks(q) * (dk ** -0.5)
    k = to_chunks(k)
    v = to_chunks(v)
    b = jnp.cumsum(to_chunks(gk), axis=3)
    causal = jnp.tril(jnp.ones((CHUNK, CHUNK), bool))[:, :, None]

    def step(S, inp):
        q_n, k_n, v_n, b_n = inp
        diff = b_n[:, :, :, None, :] - b_n[:, :, None, :, :]
        decay = jnp.exp(jnp.where(causal, diff, -jnp.inf))
        A = jnp.einsum('bhcsd,bhsd->bhcs', q_n[:, :, :, None, :] * decay, k_n)
        o = jnp.einsum('bhcd,bhde->bhce', q_n * jnp.exp(b_n), S) + jnp.einsum('bhcs,bhse->bhce', A, v_n)
        b_last = b_n[:, :, -1]
        S = S * jnp.exp(b_last)[..., None] + jnp.einsum(
            'bhcd,bhce->bhde', k_n * jnp.exp(b_last[:, :, None, :] - b_n), v_n)
        return S, o

    xs = tuple(jnp.moveaxis(a, 2, 0) for a in (q, k, v, b))
    S0 = jnp.zeros((B, H, dk, dv), jnp.float32)
    _, o = lax.scan(step, S0, xs)
    return from_scan_chunks(o)


def gdn_mixer(x, w_in, conv_w, a_log, dt_bias, norm_w, w_out):
    B, T, _ = x.shape
    f32 = jnp.float32
    p = x @ w_in
    qkv = causal_conv_silu(p[..., :GDN_QKV], conv_w)
    rest = p[..., GDN_QKV:]
    gate = rest[..., :GDN_VAL].reshape(B, T, GDN_HEADS, GDN_DV)
    b_logit = rest[..., GDN_VAL:GDN_VAL + GDN_HEADS].astype(f32)
    a_logit = rest[..., GDN_VAL + GDN_HEADS:].astype(f32)
    q = l2norm(qkv[..., :GDN_KEY].reshape(B, T, GDN_HEADS, GDN_DK).astype(f32))
    k = l2norm(qkv[..., GDN_KEY:2 * GDN_KEY].reshape(B, T, GDN_HEADS, GDN_DK).astype(f32))
    v = qkv[..., 2 * GDN_KEY:].reshape(B, T, GDN_HEADS, GDN_DV).astype(f32)
    beta = jax.nn.sigmoid(b_logit)
    g = -jnp.exp(a_log.astype(f32)) * jax.nn.softplus(a_logit + dt_bias.astype(f32))
    o = gated_delta_rule(q, k, v, g, beta)
    o = gated_rmsnorm(o, norm_w, gate)
    return o.reshape(B, T, GDN_VAL).astype(x.dtype) @ w_out


def gla_mixer(x, w_in, w_gk_up, b_gk, norm_w, w_out):
    B, T, _ = x.shape
    f32 = jnp.float32
    p = x @ w_in
    q = p[..., :GLA_KEY].reshape(B, T, GLA_HEADS, GLA_DK).astype(f32)
    k = p[..., GLA_KEY:2 * GLA_KEY].reshape(B, T, GLA_HEADS, GLA_DK).astype(f32)
    v = p[..., 2 * GLA_KEY:2 * GLA_KEY + GLA_VAL].reshape(B, T, GLA_HEADS, GLA_DV).astype(f32)
    gate = p[..., 2 * GLA_KEY + GLA_VAL:2 * GLA_KEY + 2 * GLA_VAL].reshape(B, T, GLA_HEADS, GLA_DV)
    gk_low = p[..., 2 * GLA_KEY + 2 * GLA_VAL:]
    gk_logit = (gk_low @ w_gk_up).astype(f32) + b_gk.astype(f32)
    gk = (jax.nn.log_sigmoid(gk_logit) / GLA_GATE_NORM).reshape(B, T, GLA_HEADS, GLA_DK)
    o = gla_chunked(q, k, v, gk)
    o = gated_rmsnorm(o, norm_w, gate)
    return o.reshape(B, T, GLA_VAL).astype(x.dtype) @ w_out


def setup_inputs(seed: int = 0) -> dict:
    key = jax.random.key(seed)
    ks = jax.random.split(key, 16)
    f32 = jnp.float32

    def nrm(k, shape, s):
        return jax.random.normal(k, shape, f32) * s

    NA, NB, D = N_GDN_LAYERS, N_GLA_LAYERS, D_MODEL
    x = nrm(ks[0], (BATCH, SEQ, D), 1.0)
    gdn_col = jnp.concatenate([jnp.ones((2 * GDN_KEY,), f32), jnp.full((GDN_VAL,), DEEP_BETA, f32),
                               jnp.ones((GDN_VAL + 2 * GDN_HEADS,), f32)])
    gdn_w_in = nrm(ks[1], (NA, D, GDN_IN), D ** -0.5) * gdn_col
    gdn_conv_w = nrm(ks[2], (NA, CONV_K, GDN_QKV), CONV_K ** -0.5)
    gdn_a_log = jnp.log(jax.random.uniform(ks[3], (NA, GDN_HEADS), f32, 1.0, 16.0))
    dt = jnp.exp(jax.random.uniform(ks[4], (NA, GDN_HEADS), f32, math.log(1e-3), math.log(1e-1)))
    gdn_dt_bias = dt + jnp.log(-jnp.expm1(-dt))
    gdn_norm_w = 1.0 + nrm(ks[5], (NA, GDN_DV), 0.02)
    gdn_w_out = nrm(ks[6], (NA, GDN_VAL, D), (GDN_VAL ** -0.5) * DEEP_BETA)

    gla_col = jnp.concatenate([jnp.ones((2 * GLA_KEY,), f32), jnp.full((GLA_VAL,), DEEP_BETA, f32),
                               jnp.ones((GLA_VAL + GLA_RANK,), f32)])
    gla_w_in = nrm(ks[7], (NB, D, GLA_IN), D ** -0.5) * gla_col
    gla_w_gk_up = nrm(ks[8], (NB, GLA_RANK, GLA_KEY), GLA_RANK ** -0.5)
    gla_b_gk = nrm(ks[9], (NB, GLA_KEY), 0.1)
    gla_norm_w = 1.0 + nrm(ks[10], (NB, GLA_DV), 0.02)
    gla_w_out = nrm(ks[11], (NB, GLA_VAL, D), (GLA_VAL ** -0.5) * DEEP_BETA)

    ln_g = 1.0 + nrm(ks[12], (DEPTH, D), 0.02)
    ln_b = nrm(ks[13], (DEPTH, D), 0.02)
    return {"x": x, "gdn_w_in": gdn_w_in, "gdn_conv_w": gdn_conv_w, "gdn_a_log": gdn_a_log,
            "gdn_dt_bias": gdn_dt_bias, "gdn_norm_w": gdn_norm_w, "gdn_w_out": gdn_w_out,
            "gla_w_in": gla_w_in, "gla_w_gk_up": gla_w_gk_up, "gla_b_gk": gla_b_gk,
            "gla_norm_w": gla_norm_w, "gla_w_out": gla_w_out, "ln_g": ln_g, "ln_b": ln_b}


def reference(x, gdn_w_in, gdn_conv_w, gdn_a_log, gdn_dt_bias, gdn_norm_w, gdn_w_out,
              gla_w_in, gla_w_gk_up, gla_b_gk, gla_norm_w, gla_w_out, ln_g, ln_b):
    for i in range(DEPTH):
        j = i // N_MIXERS
        if i % N_MIXERS == 0:
            y = gdn_mixer(x, gdn_w_in[j], gdn_conv_w[j], gdn_a_log[j], gdn_dt_bias[j],
                          gdn_norm_w[j], gdn_w_out[j])
        else:
            y = gla_mixer(x, gla_w_in[j], gla_w_gk_up[j], gla_b_gk[j], gla_norm_w[j], gla_w_out[j])
        x = layer_norm(DEEP_ALPHA * x + y, ln_g[i], ln_b[i])
    return x
```

```python
import functools

import jax
import jax.numpy as jnp
from jax import lax
from jax.experimental import pallas as pl
from jax.experimental.pallas import tpu as pltpu

F32 = jnp.float32
BF16 = jnp.bfloat16

D_MODEL = 1024
DEPTH = 4
CHUNK = 64
CHUNK_SHIFT = 6
CONV_K = 4

GDN_HEADS = 8
GDN_DK = 128
GDN_DV = 256
GDN_KEY = GDN_HEADS * GDN_DK
GDN_VAL = GDN_HEADS * GDN_DV
GDN_QKV = 2 * GDN_KEY + GDN_VAL

GLA_HEADS = 4
GLA_DK = 128
GLA_DV = 256
GLA_KEY = GLA_HEADS * GLA_DK
GLA_VAL = GLA_HEADS * GLA_DV
GLA_RANK = 16
GLA_GATE_NORM = 16.0

DEEP_ALPHA = (2.0 * DEPTH) ** 0.25
LN_EPS = 1e-5
RMS_EPS = 1e-6
L2_EPS = 1e-6

V7X_MXU_WIDTH = 256
V7X_SUBLANES = 8
VMEM_LIMIT_BYTES = 56 * 1024 * 1024

IN_ROWS = 256
REC_ROWS = 256
OUT_ROWS = 512
COL_TILE = V7X_MXU_WIDTH
SUB = 16
NEG_BIG = -1e30


def _mm(a, b):
  return jnp.dot(a.astype(BF16), b.astype(BF16), preferred_element_type=F32)


def _mm_nt(a, b):
  return lax.dot_general(a.astype(BF16), b.astype(BF16),
                         (((1,), (1,)), ((), ())), preferred_element_type=F32)


def _mm_tn(a, b):
  return lax.dot_general(a.astype(BF16), b.astype(BF16),
                         (((0,), (0,)), ((), ())), preferred_element_type=F32)


def _split3(x):
  hi = x.astype(BF16)
  r1 = x - hi.astype(F32)
  mid = r1.astype(BF16)
  lo = (r1 - mid.astype(F32)).astype(BF16)
  return hi, mid, lo


def _sigmoid(x):
  return 1.0 / (1.0 + jnp.exp(-x))


def _softplus(x):
  return jnp.maximum(x, 0.0) + jnp.log1p(jnp.exp(-jnp.abs(x)))


def _silu(x):
  return x * _sigmoid(x)


def _chunk_tri(n, lower):
  r = lax.broadcasted_iota(jnp.int32, (n, n), 0)
  c = lax.broadcasted_iota(jnp.int32, (n, n), 1)
  same = jnp.right_shift(r, CHUNK_SHIFT) == jnp.right_shift(c, CHUNK_SHIFT)
  tri = (c <= r) if lower else (r <= c)
  return jnp.where(same & tri, 1.0, 0.0).astype(BF16)


def _gated_rmsnorm(o, gate, nw):
  ms = jnp.mean(o * o, axis=-1, keepdims=True)
  return o * lax.rsqrt(ms + RMS_EPS) * nw * _silu(gate)


def _gdn_in_kernel(x_ref, wqkv_ref, wgate_ref, wba_ref, wbat_ref, convw_ref,
                   alog_ref, dtb_ref, alogt_ref, dtbt_ref,
                   q_ref, k_ref, v_ref, gate_ref, beta_ref, gcc_ref, gcr_ref,
                   pbuf):
  tm = IN_ROWS
  halo = V7X_SUBLANES

  @pl.when(pl.program_id(1) == 0)
  def _():
    pbuf[0:halo, :] = jnp.zeros((halo, GDN_QKV), F32)

  xb = x_ref[0].astype(BF16)

  for c in range(GDN_QKV // COL_TILE):
    cs = slice(c * COL_TILE, (c + 1) * COL_TILE)
    acc = jnp.dot(xb, wqkv_ref[:, cs], preferred_element_type=F32)
    pbuf[halo:halo + tm, cs] = acc
    cw = convw_ref[:, cs]
    y = acc * cw[CONV_K - 1:CONV_K]
    for j in range(CONV_K - 1):
      off = halo - (CONV_K - 1) + j
      y = y + pbuf[off:off + tm, cs] * cw[j:j + 1]
    y = _silu(y)
    col0 = c * COL_TILE
    if col0 < 2 * GDN_KEY:
      parts = []
      for j in range(COL_TILE // GDN_DK):
        t = y[:, j * GDN_DK:(j + 1) * GDN_DK]
        n = t * lax.rsqrt(jnp.sum(t * t, axis=-1, keepdims=True) + L2_EPS)
        parts.append(n)
      y = jnp.concatenate(parts, axis=1)
      if col0 < GDN_KEY:
        q_ref[0, :, cs] = y * (GDN_DK ** -0.5)
      else:
        k_ref[0, :, col0 - GDN_KEY:col0 - GDN_KEY + COL_TILE] = y
    else:
      v_ref[0, :, col0 - 2 * GDN_KEY:col0 - 2 * GDN_KEY + COL_TILE] = y
  pbuf[0:halo, :] = pbuf[tm:tm + halo, :]

  for c in range(GDN_VAL // COL_TILE):
    cs = slice(c * COL_TILE, (c + 1) * COL_TILE)
    gate_ref[0, :, cs] = jnp.dot(xb, wgate_ref[:, cs], preferred_element_type=F32)

  ba = jnp.dot(xb, wba_ref[...], preferred_element_type=F32)
  beta_ref[0] = _sigmoid(ba[:, :GDN_HEADS])
  g = -jnp.exp(alog_ref[...]) * _softplus(ba[:, GDN_HEADS:] + dtb_ref[...])
  lmat = _chunk_tri(tm, lower=True)
  gc = jnp.zeros((tm, GDN_HEADS), F32)
  for part in _split3(g):
    gc = gc + jnp.dot(lmat, part, preferred_element_type=F32)
  gcc_ref[0] = gc

  bat = lax.dot_general(wbat_ref[...], xb, (((1,), (1,)), ((), ())),
                        preferred_element_type=F32)
  gt = -jnp.exp(alogt_ref[...]) * _softplus(bat[GDN_HEADS:, :] + dtbt_ref[...])
  umat = _chunk_tri(tm, lower=False)
  gct = jnp.zeros((GDN_HEADS, tm), F32)
  for part in _split3(gt):
    gct = gct + jnp.dot(part, umat, preferred_element_type=F32)
  for c in range(tm // CHUNK):
    gcr_ref[0, c] = gct[:, c * CHUNK:(c + 1) * CHUNK]


def _gdn_in_proj(x, wqkv, wgate, wba, wbat, convw, alog, dtb):
  B, T, D = x.shape
  tm = IN_ROWS
  H = GDN_HEADS
  const = lambda b, t: (0, 0)
  return pl.pallas_call(
      _gdn_in_kernel,
      grid=(B, T // tm),
      in_specs=[
          pl.BlockSpec((1, tm, D), lambda b, t: (b, t, 0)),
          pl.BlockSpec((D, GDN_QKV), const),
          pl.BlockSpec((D, GDN_VAL), const),
          pl.BlockSpec((D, 2 * H), const),
          pl.BlockSpec((2 * H, D), const),
          pl.BlockSpec((CONV_K, GDN_QKV), const),
          pl.BlockSpec((1, H), const),
          pl.BlockSpec((1, H), const),
          pl.BlockSpec((H, 1), const),
          pl.BlockSpec((H, 1), const),
      ],
      out_specs=[
          pl.BlockSpec((1, tm, GDN_KEY), lambda b, t: (b, t, 0)),
          pl.BlockSpec((1, tm, GDN_KEY), lambda b, t: (b, t, 0)),
          pl.BlockSpec((1, tm, GDN_VAL), lambda b, t: (b, t, 0)),
          pl.BlockSpec((1, tm, GDN_VAL), lambda b, t: (b, t, 0)),
          pl.BlockSpec((1, tm, H), lambda b, t: (b, t, 0)),
          pl.BlockSpec((1, tm, H), lambda b, t: (b, t, 0)),
          pl.BlockSpec((1, tm // CHUNK, H, CHUNK), lambda b, t: (b, t, 0, 0)),
      ],
      out_shape=[
          jax.ShapeDtypeStruct((B, T, GDN_KEY), F32),
          jax.ShapeDtypeStruct((B, T, GDN_KEY), F32),
          jax.ShapeDtypeStruct((B, T, GDN_VAL), F32),
          jax.ShapeDtypeStruct((B, T, GDN_VAL), F32),
          jax.ShapeDtypeStruct((B, T, H), F32),
          jax.ShapeDtypeStruct((B, T, H), F32),
          jax.ShapeDtypeStruct((B, T // CHUNK, H, CHUNK), F32),
      ],
      scratch_shapes=[pltpu.VMEM((tm + V7X_SUBLANES, GDN_QKV), F32)],
      compiler_params=pltpu.CompilerParams(
          dimension_semantics=("arbitrary", "arbitrary"),
          vmem_limit_bytes=VMEM_LIMIT_BYTES),
      name="gdn_in_proj",
  )(x, wqkv, wgate, wba, wbat, convw, alog, dtb, alog.reshape(H, 1),
    dtb.reshape(H, 1))


def _unit_lower_inverse_minus_eye(a, row, col):
  same16 = jnp.right_shift(row, 4) == jnp.right_shift(col, 4)
  same32 = jnp.right_shift(row, 5) == jnp.right_shift(col, 5)
  n = jnp.where(same16, -a, 0.0)
  p = n
  s = n
  for _ in range(3):
    s = _mm(s, s)
    p = p + s + _mm(p, s)
  for mask in (same32 & jnp.logical_not(same16), jnp.logical_not(same32)):
    e = jnp.where(mask, a, 0.0)
    f = e + _mm(p, e)
    p = p - f - _mm(f, p)
  return p


def _gdn_chunk(q, k, v, beta, gcc, gcr, s_state, row, col):
  incl = row >= col
  strict = row > col
  decay = jnp.where(incl, jnp.exp(jnp.where(incl, gcc - gcr, 0.0)), 0.0)
  eg = jnp.exp(gcc)
  g_last = gcc[CHUNK - 1:CHUNK, :]
  kb = k * beta
  a = jnp.where(strict, _mm_nt(kb, k) * decay, 0.0)
  c = _unit_lower_inverse_minus_eye(a, row, col)
  rhs = jnp.concatenate([v * beta, kb * eg], axis=1)
  uw = rhs + _mm(c, rhs)
  u = uw[:, :GDN_DV]
  w = uw[:, GDN_DV:]
  qk = _mm_nt(q, k) * decay
  r = _mm(jnp.concatenate([w, q * eg], axis=0), s_state)
  v_new = u - r[:CHUNK]
  o = r[CHUNK:] + _mm(qk, v_new)
  s_new = s_state * jnp.exp(g_last) + _mm_tn(k * jnp.exp(g_last - gcc), v_new)
  return o, s_new


def _gdn_rec_kernel(q_ref, k_ref, v_ref, gate_ref, beta_ref, gcc_ref, gcr_ref,
                    nw_ref, o_ref, s_ref):
  @pl.when(pl.program_id(1) == 0)
  def _():
    s_ref[...] = jnp.zeros(s_ref.shape, F32)

  row = lax.broadcasted_iota(jnp.int32, (CHUNK, CHUNK), 0)
  col = lax.broadcasted_iota(jnp.int32, (CHUNK, CHUNK), 1)
  nw = nw_ref[...]

  def body(c, carry):
    r0 = pl.multiple_of(c * CHUNK, CHUNK)
    rows = pl.ds(r0, CHUNK)
    beta_all = beta_ref[0, rows, :]
    gcc_all = gcc_ref[0, rows, :]
    gcr_all = gcr_ref[0, c]
    for h in range(GDN_HEADS):
      ks = slice(h * GDN_DK, (h + 1) * GDN_DK)
      vs = slice(h * GDN_DV, (h + 1) * GDN_DV)
      o, s_new = _gdn_chunk(
          q_ref[0, rows, ks], k_ref[0, rows, ks], v_ref[0, rows, vs],
          beta_all[:, h:h + 1], gcc_all[:, h:h + 1], gcr_all[h:h + 1, :],
          s_ref[h], row, col)
      s_ref[h] = s_new
      o_ref[0, rows, vs] = _gated_rmsnorm(o, gate_ref[0, rows, vs], nw).astype(BF16)
    return carry

  lax.fori_loop(0, REC_ROWS // CHUNK, body, 0)


def _gdn_recurrence(q, k, v, gate, beta, gcc, gcr, norm_w):
  B, T, _ = q.shape
  tt = REC_ROWS
  H = GDN_HEADS
  rowmap = lambda b, t: (b, t, 0)
  return pl.pallas_call(
      _gdn_rec_kernel,
      grid=(B, T // tt),
      in_specs=[
          pl.BlockSpec((1, tt, GDN_KEY), rowmap),
          pl.BlockSpec((1, tt, GDN_KEY), rowmap),
          pl.BlockSpec((1, tt, GDN_VAL), rowmap),
          pl.BlockSpec((1, tt, GDN_VAL), rowmap),
          pl.BlockSpec((1, tt, H), rowmap),
          pl.BlockSpec((1, tt, H), rowmap),
          pl.BlockSpec((1, tt // CHUNK, H, CHUNK), lambda b, t: (b, t, 0, 0)),
          pl.BlockSpec((1, GDN_DV), lambda b, t: (0, 0)),
      ],
      out_specs=pl.BlockSpec((1, tt, GDN_VAL), rowmap),
      out_shape=jax.ShapeDtypeStruct((B, T, GDN_VAL), BF16),
      scratch_shapes=[pltpu.VMEM((H, GDN_DK, GDN_DV), F32)],
      compiler_params=pltpu.CompilerParams(
          dimension_semantics=("arbitrary", "arbitrary"),
          vmem_limit_bytes=VMEM_LIMIT_BYTES),
      name="gdn_recurrence",
  )(q, k, v, gate, beta, gcc, gcr, norm_w.reshape(1, GDN_DV))


def _gla_in_kernel(x_ref, wqkv_ref, wgate_ref, wlow_ref, wup_ref, bgk_ref,
                   q_ref, k_ref, v_ref, gate_ref, b_ref):
  tm = IN_ROWS
  xb = x_ref[0].astype(BF16)
  for c in range((2 * GLA_KEY + GLA_VAL) // COL_TILE):
    cs = slice(c * COL_TILE, (c + 1) * COL_TILE)
    acc = jnp.dot(xb, wqkv_ref[:, cs], preferred_element_type=F32)
    col0 = c * COL_TILE
    if col0 < GLA_KEY:
      q_ref[0, :, cs] = acc * (GLA_DK ** -0.5)
    elif col0 < 2 * GLA_KEY:
      k_ref[0, :, col0 - GLA_KEY:col0 - GLA_KEY + COL_TILE] = acc
    else:
      v_ref[0, :, col0 - 2 * GLA_KEY:col0 - 2 * GLA_KEY + COL_TILE] = acc
  for c in range(GLA_VAL // COL_TILE):
    cs = slice(c * COL_TILE, (c + 1) * COL_TILE)
    gate_ref[0, :, cs] = jnp.dot(xb, wgate_ref[:, cs], preferred_element_type=F32)

  low = jnp.dot(xb, wlow_ref[...], preferred_element_type=F32)
  logit = _mm(low, wup_ref[...]) + bgk_ref[...]
  gk = (jnp.minimum(logit, 0.0) - jnp.log1p(jnp.exp(-jnp.abs(logit)))) * (1.0 / GLA_GATE_NORM)
  lmat = _chunk_tri(CHUNK, lower=True)
  for c in range(tm // CHUNK):
    rs = slice(c * CHUNK, (c + 1) * CHUNK)
    acc = jnp.zeros((CHUNK, GLA_KEY), F32)
    for part in _split3(gk[rs]):
      acc = acc + jnp.dot(lmat, part, preferred_element_type=F32)
    b_ref[0, rs, :] = acc


def _gla_in_proj(x, wqkv, wgate, wlow, wup, bgk):
  B, T, D = x.shape
  tm = IN_ROWS
  const = lambda b, t: (0, 0)
  rowmap = lambda b, t: (b, t, 0)
  return pl.pallas_call(
      _gla_in_kernel,
      grid=(B, T // tm),
      in_specs=[
          pl.BlockSpec((1, tm, D), rowmap),
          pl.BlockSpec((D, 2 * GLA_KEY + GLA_VAL), const),
          pl.BlockSpec((D, GLA_VAL), const),
          pl.BlockSpec((D, GLA_RANK), const),
          pl.BlockSpec((GLA_RANK, GLA_KEY), const),
          pl.BlockSpec((1, GLA_KEY), const),
      ],
      out_specs=[
          pl.BlockSpec((1, tm, GLA_KEY), rowmap),
          pl.BlockSpec((1, tm, GLA_KEY), rowmap),
          pl.BlockSpec((1, tm, GLA_VAL), rowmap),
          pl.BlockSpec((1, tm, GLA_VAL), rowmap),
          pl.BlockSpec((1, tm, GLA_KEY), rowmap),
      ],
      out_shape=[
          jax.ShapeDtypeStruct((B, T, GLA_KEY), F32),
          jax.ShapeDtypeStruct((B, T, GLA_KEY), F32),
          jax.ShapeDtypeStruct((B, T, GLA_VAL), F32),
          jax.ShapeDtypeStruct((B, T, GLA_VAL), F32),
          jax.ShapeDtypeStruct((B, T, GLA_KEY), F32),
      ],
      compiler_params=pltpu.CompilerParams(
          dimension_semantics=("arbitrary", "arbitrary"),
          vmem_limit_bytes=VMEM_LIMIT_BYTES),
      name="gla_in_proj",
  )(x, wqkv, wgate, wlow, wup, bgk.reshape(1, GLA_KEY))


def _gla_chunk(q, k, v, b, st_state):
  b_last = b[CHUNK - 1:CHUNK, :]
  o_inter = _mm_nt(q * jnp.exp(b), st_state)
  k_dec = k * jnp.exp(b_last - b)

  rows16 = lax.broadcasted_iota(jnp.int32, (SUB, 1), 0)
  rows64 = lax.broadcasted_iota(jnp.int32, (CHUNK, 1), 0)
  lane64 = lax.broadcasted_iota(jnp.int32, (SUB, CHUNK), 1)
  bands = []
  for i in range(CHUNK // SUB):
    r0 = i * SUB
    qi = q[r0:r0 + SUB]
    ki = k[r0:r0 + SUB]
    bi = b[r0:r0 + SUB]
    band = jnp.zeros((SUB, CHUNK), F32)
    for s in range(SUB):
      live = rows16 >= s
      d = jnp.where(live, bi - bi[s:s + 1], 0.0)
      p = jnp.where(live, qi * ki[s:s + 1] * jnp.exp(d), 0.0)
      band = jnp.where(lane64 == r0 + s, jnp.sum(p, axis=-1, keepdims=True), band)
    if i > 0:
      ref = b[r0:r0 + 1]
      qh = qi * jnp.exp(bi - ref)
      earlier = rows64 < r0
      kh = jnp.where(earlier, k * jnp.exp(jnp.where(earlier, ref - b, 0.0)), 0.0)
      band = band + _mm_nt(qh, kh)
    bands.append(band)
  a = jnp.concatenate(bands, axis=0)
  o = o_inter + _mm(a, v)
  st_new = st_state * jnp.exp(b_last) + _mm_tn(v, k_dec)
  return o, st_new


def _gla_rec_kernel(q_ref, k_ref, v_ref, gate_ref, b_ref, nw_ref, o_ref, st_ref):
  @pl.when(pl.program_id(1) == 0)
  def _():
    st_ref[...] = jnp.zeros(st_ref.shape, F32)

  nw = nw_ref[...]

  def body(c, carry):
    r0 = pl.multiple_of(c * CHUNK, CHUNK)
    rows = pl.ds(r0, CHUNK)
    for h in range(GLA_HEADS):
      ks = slice(h * GLA_DK, (h + 1) * GLA_DK)
      vs = slice(h * GLA_DV, (h + 1) * GLA_DV)
      o, st_new = _gla_chunk(q_ref[0, rows, ks], k_ref[0, rows, ks],
                             v_ref[0, rows, vs], b_ref[0, rows, ks], st_ref[h])
      st_ref[h] = st_new
      o_ref[0, rows, vs] = _gated_rmsnorm(o, gate_ref[0, rows, vs], nw).astype(BF16)
    return carry

  lax.fori_loop(0, REC_ROWS // CHUNK, body, 0)


def _gla_recurrence(q, k, v, gate, b, norm_w):
  B, T, _ = q.shape
  tt = REC_ROWS
  rowmap = lambda bb, t: (bb, t, 0)
  return pl.pallas_call(
      _gla_rec_kernel,
      grid=(B, T // tt),
      in_specs=[
          pl.BlockSpec((1, tt, GLA_KEY), rowmap),
          pl.BlockSpec((1, tt, GLA_KEY), rowmap),
          pl.BlockSpec((1, tt, GLA_VAL), rowmap),
          pl.BlockSpec((1, tt, GLA_VAL), rowmap),
          pl.BlockSpec((1, tt, GLA_KEY), rowmap),
          pl.BlockSpec((1, GLA_DV), lambda bb, t: (0, 0)),
      ],
      out_specs=pl.BlockSpec((1, tt, GLA_VAL), rowmap),
      out_shape=jax.ShapeDtypeStruct((B, T, GLA_VAL), BF16),
      scratch_shapes=[pltpu.VMEM((GLA_HEADS, GLA_DV, GLA_DK), F32)],
      compiler_params=pltpu.CompilerParams(
          dimension_semantics=("arbitrary", "arbitrary"),
          vmem_limit_bytes=VMEM_LIMIT_BYTES),
      name="gla_recurrence",
  )(q, k, v, gate, b, norm_w.reshape(1, GLA_DV))


def _out_kernel(o_ref, x_ref, w_ref, g_ref, b_ref, y_ref):
  y = jnp.dot(o_ref[...], w_ref[...], preferred_element_type=F32)
  z = DEEP_ALPHA * x_ref[...] + y
  mu = jnp.mean(z, axis=-1, keepdims=True)
  zc = z - mu
  var = jnp.mean(zc * zc, axis=-1, keepdims=True)
  y_ref[...] = zc * lax.rsqrt(var + LN_EPS) * g_ref[...] + b_ref[...]


def _out_proj_norm(o, x, w_out, ln_g, ln_b):
  n, dv = o.shape
  tm = OUT_ROWS
  const = lambda i: (0, 0)
  return pl.pallas_call(
      _out_kernel,
      grid=(n // tm,),
      in_specs=[
          pl.BlockSpec((tm, dv), lambda i: (i, 0)),
          pl.BlockSpec((tm, D_MODEL), lambda i: (i, 0)),
          pl.BlockSpec((dv, D_MODEL), const),
          pl.BlockSpec((1, D_MODEL), const),
          pl.BlockSpec((1, D_MODEL), const),
      ],
      out_specs=pl.BlockSpec((tm, D_MODEL), lambda i: (i, 0)),
      out_shape=jax.ShapeDtypeStruct((n, D_MODEL), F32),
      compiler_params=pltpu.CompilerParams(
          dimension_semantics=("arbitrary",),
          vmem_limit_bytes=VMEM_LIMIT_BYTES),
      name="out_proj_norm",
  )(o, x, w_out, ln_g.reshape(1, D_MODEL), ln_b.reshape(1, D_MODEL))


def _gdn_layer(x, w_in, conv_w, a_log, dt_bias, norm_w, w_out, ln_g, ln_b):
  B, T, D = x.shape
  H = GDN_HEADS
  wb = w_in.astype(BF16)
  wqkv = wb[:, :GDN_QKV]
  wgate = wb[:, GDN_QKV:GDN_QKV + GDN_VAL]
  wba = wb[:, GDN_QKV + GDN_VAL:]
  q, k, v, gate, beta, gcc, gcr = _gdn_in_proj(
      x, wqkv, wgate, wba, wba.T, conv_w, a_log.reshape(1, H), dt_bias.reshape(1, H))
  o = _gdn_recurrence(q, k, v, gate, beta, gcc, gcr, norm_w)
  y = _out_proj_norm(o.reshape(B * T, GDN_VAL), x.reshape(B * T, D),
                     w_out.astype(BF16), ln_g, ln_b)
  return y.reshape(B, T, D)


def _gla_layer(x, w_in, w_gk_up, b_gk, norm_w, w_out, ln_g, ln_b):
  B, T, D = x.shape
  wb = w_in.astype(BF16)
  nqkv = 2 * GLA_KEY + GLA_VAL
  q, k, v, gate, b = _gla_in_proj(
      x, wb[:, :nqkv], wb[:, nqkv:nqkv + GLA_VAL], wb[:, nqkv + GLA_VAL:],
      w_gk_up.astype(BF16), b_gk)
  o = _gla_recurrence(q, k, v, gate, b, norm_w)
  y = _out_proj_norm(o.reshape(B * T, GLA_VAL), x.reshape(B * T, D),
                     w_out.astype(BF16), ln_g, ln_b)
  return y.reshape(B, T, D)


def kernel(x, gdn_w_in, gdn_conv_w, gdn_a_log, gdn_dt_bias, gdn_norm_w, gdn_w_out,
           gla_w_in, gla_w_gk_up, gla_b_gk, gla_norm_w, gla_w_out, ln_g, ln_b):
  for i in range(DEPTH):
    j = i // 2
    if i % 2 == 0:
      x = _gdn_layer(x, gdn_w_in[j], gdn_conv_w[j], gdn_a_log[j], gdn_dt_bias[j],
                     gdn_norm_w[j], gdn_w_out[j], ln_g[i], ln_b[i])
    else:
      x = _gla_layer(x, gla_w_in[j], gla_w_gk_up[j], gla_b_gk[j], gla_norm_w[j],
                     gla_w_out[j], ln_g[i], ln_b[i])
  return x
```

```python
import functools

import jax
import jax.numpy as jnp
from jax import lax
from jax.experimental import pallas as pl
from jax.experimental.pallas import tpu as pltpu

F32 = jnp.float32
BF16 = jnp.bfloat16

D_MODEL = 1024
DEPTH = 4
CHUNK = 64
CHUNK_SHIFT = 6
CONV_K = 4

GDN_HEADS = 8
GDN_DK = 128
GDN_DV = 256
GDN_KEY = GDN_HEADS * GDN_DK
GDN_VAL = GDN_HEADS * GDN_DV
GDN_QKV = 2 * GDN_KEY + GDN_VAL

GLA_HEADS = 4
GLA_DK = 128
GLA_DV = 256
GLA_KEY = GLA_HEADS * GLA_DK
GLA_VAL = GLA_HEADS * GLA_DV
GLA_RANK = 16
GLA_GATE_NORM = 16.0

DEEP_ALPHA = (2.0 * DEPTH) ** 0.25
LN_EPS = 1e-5
RMS_EPS = 1e-6
L2_EPS = 1e-6

V7X_MXU_WIDTH = 256
V7X_SUBLANES = 8
VMEM_LIMIT_BYTES = 56 * 1024 * 1024

IN_ROWS = 256
REC_ROWS = 256
OUT_ROWS = 512
COL_TILE = V7X_MXU_WIDTH
SUB = 16
NEG_BIG = -1e30


def _mm(a, b):
  return jnp.dot(a.astype(BF16), b.astype(BF16), preferred_element_type=F32)


def _mm_nt(a, b):
  return lax.dot_general(a.astype(BF16), b.astype(BF16),
                         (((1,), (1,)), ((), ())), preferred_element_type=F32)


def _mm_tn(a, b):
  return lax.dot_general(a.astype(BF16), b.astype(BF16),
                         (((0,), (0,)), ((), ())), preferred_element_type=F32)


def _split3(x):
  hi = x.astype(BF16)
  r1 = x - hi.astype(F32)
  mid = r1.astype(BF16)
  lo = (r1 - mid.astype(F32)).astype(BF16)
  return hi, mid, lo


def _sigmoid(x):
  return 1.0 / (1.0 + jnp.exp(-x))


def _softplus(x):
  return jnp.maximum(x, 0.0) + jnp.log1p(jnp.exp(-jnp.abs(x)))


def _silu(x):
  return x * _sigmoid(x)


def _chunk_tri(n, lower):
  r = lax.broadcasted_iota(jnp.int32, (n, n), 0)
  c = lax.broadcasted_iota(jnp.int32, (n, n), 1)
  same = jnp.right_shift(r, CHUNK_SHIFT) == jnp.right_shift(c, CHUNK_SHIFT)
  tri = (c <= r) if lower else (r <= c)
  return jnp.where(same & tri, 1.0, 0.0).astype(BF16)


def _gated_rmsnorm(o, gate, nw):
  ms = jnp.mean(o * o, axis=-1, keepdims=True)
  return o * lax.rsqrt(ms + RMS_EPS) * nw * _silu(gate)


def _gdn_in_kernel(x_ref, wqkv_ref, wgate_ref, wba_ref, wbat_ref, convw_ref,
                   alog_ref, dtb_ref, alogt_ref, dtbt_ref,
                   q_ref, k_ref, v_ref, gate_ref, beta_ref, gcc_ref, gcr_ref,
                   pbuf):
  tm = IN_ROWS
  halo = V7X_SUBLANES

  @pl.when(pl.program_id(1) == 0)
  def _():
    pbuf[0:halo, :] = jnp.zeros((halo, GDN_QKV), F32)

  xb = x_ref[0].astype(BF16)

  for c in range(GDN_QKV // COL_TILE):
    cs = slice(c * COL_TILE, (c + 1) * COL_TILE)
    acc = jnp.dot(xb, wqkv_ref[:, cs], preferred_element_type=F32)
    pbuf[halo:halo + tm, cs] = acc
    cw = convw_ref[:, cs]
    y = acc * cw[CONV_K - 1:CONV_K]
    for j in range(CONV_K - 1):
      off = halo - (CONV_K - 1) + j
      y = y + pbuf[off:off + tm, cs] * cw[j:j + 1]
    y = _silu(y)
    col0 = c * COL_TILE
    if col0 < 2 * GDN_KEY:
      parts = []
      for j in range(COL_TILE // GDN_DK):
        t = y[:, j * GDN_DK:(j + 1) * GDN_DK]
        n = t * lax.rsqrt(jnp.sum(t * t, axis=-1, keepdims=True) + L2_EPS)
        parts.append(n)
      y = jnp.concatenate(parts, axis=1)
      if col0 < GDN_KEY:
        q_ref[0, :, cs] = y * (GDN_DK ** -0.5)
      else:
        k_ref[0, :, col0 - GDN_KEY:col0 - GDN_KEY + COL_TILE] = y
    else:
      v_ref[0, :, col0 - 2 * GDN_KEY:col0 - 2 * GDN_KEY + COL_TILE] = y
  pbuf[0:halo, :] = pbuf[tm:tm + halo, :]

  for c in range(GDN_VAL // COL_TILE):
    cs = slice(c * COL_TILE, (c + 1) * COL_TILE)
    gate_ref[0, :, cs] = jnp.dot(xb, wgate_ref[:, cs], preferred_element_type=F32)

  ba = jnp.dot(xb, wba_ref[...], preferred_element_type=F32)
  beta_ref[0] = _sigmoid(ba[:, :GDN_HEADS])
  g = -jnp.exp(alog_ref[...]) * _softplus(ba[:, GDN_HEADS:] + dtb_ref[...])
  lmat = _chunk_tri(tm, lower=True)
  gc = jnp.zeros((tm, GDN_HEADS), F32)
  for part in _split3(g):
    gc = gc + jnp.dot(lmat, part, preferred_element_type=F32)
  gcc_ref[0] = gc

  bat = lax.dot_general(wbat_ref[...], xb, (((1,), (1,)), ((), ())),
                        preferred_element_type=F32)
  gt = -jnp.exp(alogt_ref[...]) * _softplus(bat[GDN_HEADS:, :] + dtbt_ref[...])
  umat = _chunk_tri(tm, lower=False)
  gct = jnp.zeros((GDN_HEADS, tm), F32)
  for part in _split3(gt):
    gct = gct + jnp.dot(part, umat, preferred_element_type=F32)
  for c in range(tm // CHUNK):
    gcr_ref[0, c] = gct[:, c * CHUNK:(c + 1) * CHUNK]


def _gdn_in_proj(x, wqkv, wgate, wba, wbat, convw, alog, dtb):
  B, T, D = x.shape
  tm = IN_ROWS
  H = GDN_HEADS
  const = lambda b, t: (0, 0)
  return pl.pallas_call(
      _gdn_in_kernel,
      grid=(B, T // tm),
      in_specs=[
          pl.BlockSpec((1, tm, D), lambda b, t: (b, t, 0)),
          pl.BlockSpec((D, GDN_QKV), const),
          pl.BlockSpec((D, GDN_VAL), const),
          pl.BlockSpec((D, 2 * H), const),
          pl.BlockSpec((2 * H, D), const),
          pl.BlockSpec((CONV_K, GDN_QKV), const),
          pl.BlockSpec((1, H), const),
          pl.BlockSpec((1, H), const),
          pl.BlockSpec((H, 1), const),
          pl.BlockSpec((H, 1), const),
      ],
      out_specs=[
          pl.BlockSpec((1, tm, GDN_KEY), lambda b, t: (b, t, 0)),
          pl.BlockSpec((1, tm, GDN_KEY), lambda b, t: (b, t, 0)),
          pl.BlockSpec((1, tm, GDN_VAL), lambda b, t: (b, t, 0)),
          pl.BlockSpec((1, tm, GDN_VAL), lambda b, t: (b, t, 0)),
          pl.BlockSpec((1, tm, H), lambda b, t: (b, t, 0)),
          pl.BlockSpec((1, tm, H), lambda b, t: (b, t, 0)),
          pl.BlockSpec((1, tm // CHUNK, H, CHUNK), lambda b, t: (b, t, 0, 0)),
      ],
      out_shape=[
          jax.ShapeDtypeStruct((B, T, GDN_KEY), F32),
          jax.ShapeDtypeStruct((B, T, GDN_KEY), F32),
          jax.ShapeDtypeStruct((B, T, GDN_VAL), F32),
          jax.ShapeDtypeStruct((B, T, GDN_VAL), F32),
          jax.ShapeDtypeStruct((B, T, H), F32),
          jax.ShapeDtypeStruct((B, T, H), F32),
          jax.ShapeDtypeStruct((B, T // CHUNK, H, CHUNK), F32),
      ],
      scratch_shapes=[pltpu.VMEM((tm + V7X_SUBLANES, GDN_QKV), F32)],
      compiler_params=pltpu.CompilerParams(
          dimension_semantics=("arbitrary", "arbitrary"),
          vmem_limit_bytes=VMEM_LIMIT_BYTES),
      name="gdn_in_proj",
  )(x, wqkv, wgate, wba, wbat, convw, alog, dtb, alog.reshape(H, 1),
    dtb.reshape(H, 1))


def _unit_lower_inverse_minus_eye(a_list, row, col):
  same16 = jnp.right_shift(row, 4) == jnp.right_shift(col, 4)
  same32 = jnp.right_shift(row, 5) == jnp.right_shift(col, 5)
  p = [jnp.where(same16, -a, 0.0) for a in a_list]
  s = p
  for _ in range(3):
    s = [_mm(x, x) for x in s]
    ps = [_mm(x, y) for x, y in zip(p, s)]
    p = [x + y + z for x, y, z in zip(p, s, ps)]
  for mask in (same32 & jnp.logical_not(same16), jnp.logical_not(same32)):
    e = [jnp.where(mask, a, 0.0) for a in a_list]
    f = [y + _mm(x, y) for x, y in zip(p, e)]
    p = [x - y - _mm(y, x) for x, y in zip(p, f)]
  return p


def _gdn_chunk_all_heads(c, q_ref, k_ref, v_ref, gate_ref, beta_ref, gcc_ref, gcr_ref,
                         nw, o_ref, s_ref, row, col):
  heads = range(GDN_HEADS)
  r0 = pl.multiple_of(c * CHUNK, CHUNK)
  rows = pl.ds(r0, CHUNK)
  ks = [slice(h * GDN_DK, (h + 1) * GDN_DK) for h in heads]
  vs = [slice(h * GDN_DV, (h + 1) * GDN_DV) for h in heads]
  beta_all = beta_ref[0, rows, :]
  gcc_all = gcc_ref[0, rows, :]
  gcr_all = gcr_ref[0, c]
  beta = [beta_all[:, h:h + 1] for h in heads]
  gcc = [gcc_all[:, h:h + 1] for h in heads]
  incl = row >= col
  strict = row > col
  decay = [jnp.where(incl, jnp.exp(jnp.where(incl, gcc[h] - gcr_all[h:h + 1, :], 0.0)), 0.0)
           for h in heads]
  eg = [jnp.exp(x) for x in gcc]
  g_last = [x[CHUNK - 1:CHUNK, :] for x in gcc]
  k = [k_ref[0, rows, ks[h]] for h in heads]
  kb = [k[h] * beta[h] for h in heads]
  a = [jnp.where(strict, _mm_nt(kb[h], k[h]) * decay[h], 0.0) for h in heads]
  qk = [_mm_nt(q_ref[0, rows, ks[h]], k[h]) * decay[h] for h in heads]
  cm = _unit_lower_inverse_minus_eye(a, row, col)
  rhs = [jnp.concatenate([v_ref[0, rows, vs[h]] * beta[h], kb[h] * eg[h]], axis=1)
         for h in heads]
  uw = [rhs[h] + _mm(cm[h], rhs[h]) for h in heads]
  r = [_mm(jnp.concatenate([uw[h][:, GDN_DV:], q_ref[0, rows, ks[h]] * eg[h]], axis=0),
           s_ref[h]) for h in heads]
  v_new = [uw[h][:, :GDN_DV] - r[h][:CHUNK] for h in heads]
  o = [r[h][CHUNK:] + _mm(qk[h], v_new[h]) for h in heads]
  for h in heads:
    s_ref[h] = (s_ref[h] * jnp.exp(g_last[h])
                + _mm_tn(k[h] * jnp.exp(g_last[h] - gcc[h]), v_new[h]))
  for h in heads:
    o_ref[0, rows, vs[h]] = _gated_rmsnorm(o[h], gate_ref[0, rows, vs[h]], nw).astype(BF16)


def _gdn_rec_kernel(q_ref, k_ref, v_ref, gate_ref, beta_ref, gcc_ref, gcr_ref,
                    nw_ref, o_ref, s_ref):
  @pl.when(pl.program_id(1) == 0)
  def _():
    s_ref[...] = jnp.zeros(s_ref.shape, F32)

  row = lax.broadcasted_iota(jnp.int32, (CHUNK, CHUNK), 0)
  col = lax.broadcasted_iota(jnp.int32, (CHUNK, CHUNK), 1)
  nw = nw_ref[...]

  def body(c, carry):
    _gdn_chunk_all_heads(c, q_ref, k_ref, v_ref, gate_ref, beta_ref, gcc_ref, gcr_ref,
                         nw, o_ref, s_ref, row, col)
    return carry

  lax.fori_loop(0, REC_ROWS // CHUNK, body, 0)


def _gdn_recurrence(q, k, v, gate, beta, gcc, gcr, norm_w):
  B, T, _ = q.shape
  tt = REC_ROWS
  H = GDN_HEADS
  rowmap = lambda b, t: (b, t, 0)
  return pl.pallas_call(
      _gdn_rec_kernel,
      grid=(B, T // tt),
      in_specs=[
          pl.BlockSpec((1, tt, GDN_KEY), rowmap),
          pl.BlockSpec((1, tt, GDN_KEY), rowmap),
          pl.BlockSpec((1, tt, GDN_VAL), rowmap),
          pl.BlockSpec((1, tt, GDN_VAL), rowmap),
          pl.BlockSpec((1, tt, H), rowmap),
          pl.BlockSpec((1, tt, H), rowmap),
          pl.BlockSpec((1, tt // CHUNK, H, CHUNK), lambda b, t: (b, t, 0, 0)),
          pl.BlockSpec((1, GDN_DV), lambda b, t: (0, 0)),
      ],
      out_specs=pl.BlockSpec((1, tt, GDN_VAL), rowmap),
      out_shape=jax.ShapeDtypeStruct((B, T, GDN_VAL), BF16),
      scratch_shapes=[pltpu.VMEM((H, GDN_DK, GDN_DV), F32)],
      compiler_params=pltpu.CompilerParams(
          dimension_semantics=("arbitrary", "arbitrary"),
          vmem_limit_bytes=VMEM_LIMIT_BYTES),
      name="gdn_recurrence",
  )(q, k, v, gate, beta, gcc, gcr, norm_w.reshape(1, GDN_DV))


def _gla_in_kernel(x_ref, wqkv_ref, wgate_ref, wlow_ref, wup_ref, bgk_ref,
                   q_ref, k_ref, v_ref, gate_ref, b_ref):
  tm = IN_ROWS
  xb = x_ref[0].astype(BF16)
  for c in range((2 * GLA_KEY + GLA_VAL) // COL_TILE):
    cs = slice(c * COL_TILE, (c + 1) * COL_TILE)
    acc = jnp.dot(xb, wqkv_ref[:, cs], preferred_element_type=F32)
    col0 = c * COL_TILE
    if col0 < GLA_KEY:
      q_ref[0, :, cs] = acc * (GLA_DK ** -0.5)
    elif col0 < 2 * GLA_KEY:
      k_ref[0, :, col0 - GLA_KEY:col0 - GLA_KEY + COL_TILE] = acc
    else:
      v_ref[0, :, col0 - 2 * GLA_KEY:col0 - 2 * GLA_KEY + COL_TILE] = acc
  for c in range(GLA_VAL // COL_TILE):
    cs = slice(c * COL_TILE, (c + 1) * COL_TILE)
    gate_ref[0, :, cs] = jnp.dot(xb, wgate_ref[:, cs], preferred_element_type=F32)

  low = jnp.dot(xb, wlow_ref[...], preferred_element_type=F32)
  logit = _mm(low, wup_ref[...]) + bgk_ref[...]
  gk = (jnp.minimum(logit, 0.0) - jnp.log1p(jnp.exp(-jnp.abs(logit)))) * (1.0 / GLA_GATE_NORM)
  lmat = _chunk_tri(CHUNK, lower=True)
  for c in range(tm // CHUNK):
    rs = slice(c * CHUNK, (c + 1) * CHUNK)
    acc = jnp.zeros((CHUNK, GLA_KEY), F32)
    for part in _split3(gk[rs]):
      acc = acc + jnp.dot(lmat, part, preferred_element_type=F32)
    b_ref[0, rs, :] = acc


def _gla_in_proj(x, wqkv, wgate, wlow, wup, bgk):
  B, T, D = x.shape
  tm = IN_ROWS
  const = lambda b, t: (0, 0)
  rowmap = lambda b, t: (b, t, 0)
  return pl.pallas_call(
      _gla_in_kernel,
      grid=(B, T // tm),
      in_specs=[
          pl.BlockSpec((1, tm, D), rowmap),
          pl.BlockSpec((D, 2 * GLA_KEY + GLA_VAL), const),
          pl.BlockSpec((D, GLA_VAL), const),
          pl.BlockSpec((D, GLA_RANK), const),
          pl.BlockSpec((GLA_RANK, GLA_KEY), const),
          pl.BlockSpec((1, GLA_KEY), const),
      ],
      out_specs=[
          pl.BlockSpec((1, tm, GLA_KEY), rowmap),
          pl.BlockSpec((1, tm, GLA_KEY), rowmap),
          pl.BlockSpec((1, tm, GLA_VAL), rowmap),
          pl.BlockSpec((1, tm, GLA_VAL), rowmap),
          pl.BlockSpec((1, tm, GLA_KEY), rowmap),
      ],
      out_shape=[
          jax.ShapeDtypeStruct((B, T, GLA_KEY), F32),
          jax.ShapeDtypeStruct((B, T, GLA_KEY), F32),
          jax.ShapeDtypeStruct((B, T, GLA_VAL), F32),
          jax.ShapeDtypeStruct((B, T, GLA_VAL), F32),
          jax.ShapeDtypeStruct((B, T, GLA_KEY), F32),
      ],
      compiler_params=pltpu.CompilerParams(
          dimension_semantics=("arbitrary", "arbitrary"),
          vmem_limit_bytes=VMEM_LIMIT_BYTES),
      name="gla_in_proj",
  )(x, wqkv, wgate, wlow, wup, bgk.reshape(1, GLA_KEY))


def _gla_chunk_all_heads(c, q_ref, k_ref, v_ref, gate_ref, b_ref, nw, o_ref, st_ref):
  heads = range(GLA_HEADS)
  nsub = CHUNK // SUB
  r0c = pl.multiple_of(c * CHUNK, CHUNK)
  rows = pl.ds(r0c, CHUNK)
  ks = [slice(h * GLA_DK, (h + 1) * GLA_DK) for h in heads]
  vs = [slice(h * GLA_DV, (h + 1) * GLA_DV) for h in heads]
  q = [q_ref[0, rows, ks[h]] for h in heads]
  k = [k_ref[0, rows, ks[h]] for h in heads]
  b = [b_ref[0, rows, ks[h]] for h in heads]
  b_last = [x[CHUNK - 1:CHUNK, :] for x in b]
  o_inter = [_mm_nt(q[h] * jnp.exp(b[h]), st_ref[h]) for h in heads]

  rows16 = lax.broadcasted_iota(jnp.int32, (SUB, 1), 0)
  rows64 = lax.broadcasted_iota(jnp.int32, (CHUNK, 1), 0)
  lane64 = lax.broadcasted_iota(jnp.int32, (SUB, CHUNK), 1)
  bands = [[None] * nsub for _ in heads]
  for h in heads:
    for i in range(nsub):
      r0 = i * SUB
      qi = q[h][r0:r0 + SUB]
      ki = k[h][r0:r0 + SUB]
      bi = b[h][r0:r0 + SUB]
      band = jnp.zeros((SUB, CHUNK), F32)
      for s in range(SUB):
        live = rows16 >= s
        d = jnp.where(live, bi - bi[s:s + 1], 0.0)
        p = jnp.where(live, qi * ki[s:s + 1] * jnp.exp(d), 0.0)
        band = jnp.where(lane64 == r0 + s, jnp.sum(p, axis=-1, keepdims=True), band)
      bands[h][i] = band
  for i in range(1, nsub):
    r0 = i * SUB
    earlier = rows64 < r0
    for h in heads:
      ref = b[h][r0:r0 + 1]
      qh = q[h][r0:r0 + SUB] * jnp.exp(b[h][r0:r0 + SUB] - ref)
      kh = jnp.where(earlier, k[h] * jnp.exp(jnp.where(earlier, ref - b[h], 0.0)), 0.0)
      bands[h][i] = bands[h][i] + _mm_nt(qh, kh)
  o = [o_inter[h] + _mm(jnp.concatenate(bands[h], axis=0), v_ref[0, rows, vs[h]])
       for h in heads]
  for h in heads:
    st_ref[h] = (st_ref[h] * jnp.exp(b_last[h])
                 + _mm_tn(v_ref[0, rows, vs[h]], k[h] * jnp.exp(b_last[h] - b[h])))
  for h in heads:
    o_ref[0, rows, vs[h]] = _gated_rmsnorm(o[h], gate_ref[0, rows, vs[h]], nw).astype(BF16)


def _gla_rec_kernel(q_ref, k_ref, v_ref, gate_ref, b_ref, nw_ref, o_ref, st_ref):
  @pl.when(pl.program_id(1) == 0)
  def _():
    st_ref[...] = jnp.zeros(st_ref.shape, F32)

  nw = nw_ref[...]

  def body(c, carry):
    _gla_chunk_all_heads(c, q_ref, k_ref, v_ref, gate_ref, b_ref, nw, o_ref, st_ref)
    return carry

  lax.fori_loop(0, REC_ROWS // CHUNK, body, 0)


def _gla_recurrence(q, k, v, gate, b, norm_w):
  B, T, _ = q.shape
  tt = REC_ROWS
  rowmap = lambda bb, t: (bb, t, 0)
  return pl.pallas_call(
      _gla_rec_kernel,
      grid=(B, T // tt),
      in_specs=[
          pl.BlockSpec((1, tt, GLA_KEY), rowmap),
          pl.BlockSpec((1, tt, GLA_KEY), rowmap),
          pl.BlockSpec((1, tt, GLA_VAL), rowmap),
          pl.BlockSpec((1, tt, GLA_VAL), rowmap),
          pl.BlockSpec((1, tt, GLA_KEY), rowmap),
          pl.BlockSpec((1, GLA_DV), lambda bb, t: (0, 0)),
      ],
      out_specs=pl.BlockSpec((1, tt, GLA_VAL), rowmap),
      out_shape=jax.ShapeDtypeStruct((B, T, GLA_VAL), BF16),
      scratch_shapes=[pltpu.VMEM((GLA_HEADS, GLA_DV, GLA_DK), F32)],
      compiler_params=pltpu.CompilerParams(
          dimension_semantics=("arbitrary", "arbitrary"),
          vmem_limit_bytes=VMEM_LIMIT_BYTES),
      name="gla_recurrence",
  )(q, k, v, gate, b, norm_w.reshape(1, GLA_DV))


def _out_kernel(o_ref, x_ref, w_ref, g_ref, b_ref, y_ref):
  y = jnp.dot(o_ref[...], w_ref[...], preferred_element_type=F32)
  z = DEEP_ALPHA * x_ref[...] + y
  mu = jnp.mean(z, axis=-1, keepdims=True)
  zc = z - mu
  var = jnp.mean(zc * zc, axis=-1, keepdims=True)
  y_ref[...] = zc * lax.rsqrt(var + LN_EPS) * g_ref[...] + b_ref[...]


def _out_proj_norm(o, x, w_out, ln_g, ln_b):
  n, dv = o.shape
  tm = OUT_ROWS
  const = lambda i: (0, 0)
  return pl.pallas_call(
      _out_kernel,
      grid=(n // tm,),
      in_specs=[
          pl.BlockSpec((tm, dv), lambda i: (i, 0)),
          pl.BlockSpec((tm, D_MODEL), lambda i: (i, 0)),
          pl.BlockSpec((dv, D_MODEL), const),
          pl.BlockSpec((1, D_MODEL), const),
          pl.BlockSpec((1, D_MODEL), const),
      ],
      out_specs=pl.BlockSpec((tm, D_MODEL), lambda i: (i, 0)),
      out_shape=jax.ShapeDtypeStruct((n, D_MODEL), F32),
      compiler_params=pltpu.CompilerParams(
          dimension_semantics=("arbitrary",),
          vmem_limit_bytes=VMEM_LIMIT_BYTES),
      name="out_proj_norm",
  )(o, x, w_out, ln_g.reshape(1, D_MODEL), ln_b.reshape(1, D_MODEL))


def _gdn_layer(x, w_in, conv_w, a_log, dt_bias, norm_w, w_out, ln_g, ln_b):
  B, T, D = x.shape
  H = GDN_HEADS
  wb = w_in.astype(BF16)
  wqkv = wb[:, :GDN_QKV]
  wgate = wb[:, GDN_QKV:GDN_QKV + GDN_VAL]
  wba = wb[:, GDN_QKV + GDN_VAL:]
  q, k, v, gate, beta, gcc, gcr = _gdn_in_proj(
      x, wqkv, wgate, wba, wba.T, conv_w, a_log.reshape(1, H), dt_bias.reshape(1, H))
  o = _gdn_recurrence(q, k, v, gate, beta, gcc, gcr, norm_w)
  y = _out_proj_norm(o.reshape(B * T, GDN_VAL), x.reshape(B * T, D),
                     w_out.astype(BF16), ln_g, ln_b)
  return y.reshape(B, T, D)


def _gla_layer(x, w_in, w_gk_up, b_gk, norm_w, w_out, ln_g, ln_b):
  B, T, D = x.shape
  wb = w_in.astype(BF16)
  nqkv = 2 * GLA_KEY + GLA_VAL
  q, k, v, gate, b = _gla_in_proj(
      x, wb[:, :nqkv], wb[:, nqkv:nqkv + GLA_VAL], wb[:, nqkv + GLA_VAL:],
      w_gk_up.astype(BF16), b_gk)
  o = _gla_recurrence(q, k, v, gate, b, norm_w)
  y = _out_proj_norm(o.reshape(B * T, GLA_VAL), x.reshape(B * T, D),
                     w_out.astype(BF16), ln_g, ln_b)
  return y.reshape(B, T, D)


def kernel(x, gdn_w_in, gdn_conv_w, gdn_a_log, gdn_dt_bias, gdn_norm_w, gdn_w_out,
           gla_w_in, gla_w_gk_up, gla_b_gk, gla_norm_w, gla_w_out, ln_g, ln_b):
  for i in range(DEPTH):
    j = i // 2
    if i % 2 == 0:
      x = _gdn_layer(x, gdn_w_in[j], gdn_conv_w[j], gdn_a_log[j], gdn_dt_bias[j],
                     gdn_norm_w[j], gdn_w_out[j], ln_g[i], ln_b[i])
    else:
      x = _gla_layer(x, gla_w_in[j], gla_w_gk_up[j], gla_b_gk[j], gla_norm_w[j],
                     gla_w_out[j], ln_g[i], ln_b[i])
  return x
```

```python
import jax
import jax.numpy as jnp
from jax import lax
from jax.experimental import pallas as pl
from jax.experimental.pallas import tpu as pltpu

F32 = jnp.float32
BF16 = jnp.bfloat16

D_MODEL = 1024
DEPTH = 4
CHUNK = 64
CHUNK_SHIFT = 6
CONV_K = 4

GDN_HEADS = 8
GDN_DK = 128
GDN_DV = 256
GDN_KEY = GDN_HEADS * GDN_DK
GDN_VAL = GDN_HEADS * GDN_DV
GDN_QKV = 2 * GDN_KEY + GDN_VAL
GDN_PAIRS = GDN_HEADS // 2

GLA_HEADS = 4
GLA_DK = 128
GLA_DV = 256
GLA_KEY = GLA_HEADS * GLA_DK
GLA_VAL = GLA_HEADS * GLA_DV
GLA_QKV = 2 * GLA_KEY + GLA_VAL
GLA_RANK = 16
GLA_GATE_NORM = 16.0

DEEP_ALPHA = (2.0 * DEPTH) ** 0.25
LN_EPS = 1e-5
RMS_EPS = 1e-6
L2_EPS = 1e-6
LOG2E = 1.4426950408889634

V7X_MXU_WIDTH = 256
V7X_SUBLANES = 8
V7X_LANES = 128
VMEM_LIMIT_BYTES = 56 * 1024 * 1024

IN_ROWS = 256
REC_ROWS = 256
OUT_ROWS = 512
COL_TILE = V7X_MXU_WIDTH
SUB = 16
NEG_BIG = -1e30


def _mm(a, b):
  return jnp.dot(a.astype(BF16), b.astype(BF16), preferred_element_type=F32)


def _mm_nt(a, b):
  return lax.dot_general(a.astype(BF16), b.astype(BF16),
                         (((1,), (1,)), ((), ())), preferred_element_type=F32)


def _mm_tn(a, b):
  return lax.dot_general(a.astype(BF16), b.astype(BF16),
                         (((0,), (0,)), ((), ())), preferred_element_type=F32)


def _split3(x):
  hi = x.astype(BF16)
  r1 = x - hi.astype(F32)
  mid = r1.astype(BF16)
  lo = (r1 - mid.astype(F32)).astype(BF16)
  return hi, mid, lo


def _sigmoid(x):
  return 1.0 / (1.0 + jnp.exp(-x))


def _softplus(x):
  return jnp.maximum(x, 0.0) + jnp.log1p(jnp.exp(-jnp.abs(x)))


def _silu(x):
  return x * _sigmoid(x)


def _chunk_tri(n, lower):
  r = lax.broadcasted_iota(jnp.int32, (n, n), 0)
  c = lax.broadcasted_iota(jnp.int32, (n, n), 1)
  same = jnp.right_shift(r, CHUNK_SHIFT) == jnp.right_shift(c, CHUNK_SHIFT)
  tri = (c <= r) if lower else (r <= c)
  return jnp.where(same & tri, 1.0, 0.0).astype(BF16)


def _gated_rmsnorm(o, gate, nw):
  ms = jnp.mean(o * o, axis=-1, keepdims=True)
  return o * lax.rsqrt(ms + RMS_EPS) * nw * _silu(gate)


def _gdn_in_kernel(x_ref, w_ref, wat_ref, convw_ref, alog_ref, dtb_ref, alogt_ref, dtbt_ref,
                   q_ref, k_ref, v_ref, gate_ref, beta_ref, gcc_ref, gcr_ref, pbuf):
  tm = IN_ROWS
  halo = V7X_SUBLANES
  H = GDN_HEADS

  @pl.when(pl.program_id(1) == 0)
  def _():
    pbuf[0:halo, :] = jnp.zeros((halo, GDN_QKV), F32)

  xb = x_ref[0]

  def gate_cols(c):
    cs = slice(c * COL_TILE, (c + 1) * COL_TILE)
    ws = slice(GDN_QKV + c * COL_TILE, GDN_QKV + (c + 1) * COL_TILE)
    gate_ref[0, :, cs] = jnp.dot(xb, w_ref[:, ws], preferred_element_type=F32).astype(BF16)

  for c in range(GDN_QKV // COL_TILE):
    cs = slice(c * COL_TILE, (c + 1) * COL_TILE)
    acc = jnp.dot(xb, w_ref[:, cs], preferred_element_type=F32)
    pbuf[halo:halo + tm, cs] = acc
    cw = convw_ref[:, cs]
    y = acc * cw[CONV_K - 1:CONV_K]
    for j in range(CONV_K - 1):
      off = halo - (CONV_K - 1) + j
      y = y + pbuf[off:off + tm, cs] * cw[j:j + 1]
    y = _silu(y)
    col0 = c * COL_TILE
    if col0 < 2 * GDN_KEY:
      parts = []
      for j in range(COL_TILE // GDN_DK):
        t = y[:, j * GDN_DK:(j + 1) * GDN_DK]
        parts.append(t * lax.rsqrt(jnp.sum(t * t, axis=-1, keepdims=True) + L2_EPS))
      y = jnp.concatenate(parts, axis=1)
      if col0 < GDN_KEY:
        q_ref[0, :, cs] = (y * (GDN_DK ** -0.5)).astype(BF16)
      else:
        k_ref[0, :, col0 - GDN_KEY:col0 - GDN_KEY + COL_TILE] = y.astype(BF16)
    else:
      v_ref[0, :, col0 - 2 * GDN_KEY:col0 - 2 * GDN_KEY + COL_TILE] = y.astype(BF16)
    if c % 2 == 1:
      gate_cols(c // 2)
  pbuf[0:halo, :] = pbuf[tm:tm + halo, :]

  ba = jnp.dot(xb, w_ref[:, GDN_QKV + GDN_VAL:], preferred_element_type=F32)
  beta_ref[0] = _sigmoid(ba[:, :H])
  g = -jnp.exp(alog_ref[...]) * _softplus(ba[:, H:] + dtb_ref[...])
  lmat = _chunk_tri(tm, lower=True)
  gc = jnp.zeros((tm, H), F32)
  for part in _split3(g):
    gc = gc + jnp.dot(lmat, part, preferred_element_type=F32)
  gcc_ref[0] = gc

  at = lax.dot_general(wat_ref[...], xb, (((1,), (1,)), ((), ())),
                       preferred_element_type=F32)
  gt = -jnp.exp(alogt_ref[...]) * _softplus(at + dtbt_ref[...])
  umat = _chunk_tri(tm, lower=False)
  gct = jnp.zeros((H, tm), F32)
  for part in _split3(gt):
    gct = gct + jnp.dot(part, umat, preferred_element_type=F32)
  even = gct[:GDN_PAIRS]
  odd = gct[GDN_PAIRS:]
  odd_r = pltpu.roll(gct, CHUNK, 1)[GDN_PAIRS:]
  even_r = pltpu.roll(gct, tm - CHUNK, 1)[:GDN_PAIRS]
  low_half = (lax.broadcasted_iota(jnp.int32, (GDN_PAIRS, tm), 1) & (V7X_LANES - 1)) < CHUNK
  p_even = jnp.where(low_half, even, odd_r)
  p_odd = jnp.where(low_half, even_r, odd)
  for j in range(tm // V7X_LANES):
    ls = slice(j * V7X_LANES, (j + 1) * V7X_LANES)
    gcr_ref[0, 2 * j] = p_even[:, ls]
    gcr_ref[0, 2 * j + 1] = p_odd[:, ls]


def _gdn_in_proj(xb, w, wat, convw, alog, dtb, alogt, dtbt):
  B, T, D = xb.shape
  tm = IN_ROWS
  H = GDN_HEADS
  const = lambda b, t: (0, 0)
  rowmap = lambda b, t: (b, t, 0)
  return pl.pallas_call(
      _gdn_in_kernel,
      grid=(B, T // tm),
      in_specs=[
          pl.BlockSpec((1, tm, D), rowmap),
          pl.BlockSpec(w.shape, const),
          pl.BlockSpec((H, D), const),
          pl.BlockSpec((CONV_K, GDN_QKV), const),
          pl.BlockSpec((1, H), const),
          pl.BlockSpec((1, H), const),
          pl.BlockSpec((H, 1), const),
          pl.BlockSpec((H, 1), const),
      ],
      out_specs=[
          pl.BlockSpec((1, tm, GDN_KEY), rowmap),
          pl.BlockSpec((1, tm, GDN_KEY), rowmap),
          pl.BlockSpec((1, tm, GDN_VAL), rowmap),
          pl.BlockSpec((1, tm, GDN_VAL), rowmap),
          pl.BlockSpec((1, tm, H), rowmap),
          pl.BlockSpec((1, tm, H), rowmap),
          pl.BlockSpec((1, tm // CHUNK, GDN_PAIRS, V7X_LANES), lambda b, t: (b, t, 0, 0)),
      ],
      out_shape=[
          jax.ShapeDtypeStruct((B, T, GDN_KEY), BF16),
          jax.ShapeDtypeStruct((B, T, GDN_KEY), BF16),
          jax.ShapeDtypeStruct((B, T, GDN_VAL), BF16),
          jax.ShapeDtypeStruct((B, T, GDN_VAL), BF16),
          jax.ShapeDtypeStruct((B, T, H), F32),
          jax.ShapeDtypeStruct((B, T, H), F32),
          jax.ShapeDtypeStruct((B, T // CHUNK, GDN_PAIRS, V7X_LANES), F32),
      ],
      scratch_shapes=[pltpu.VMEM((tm + V7X_SUBLANES, GDN_QKV), F32)],
      compiler_params=pltpu.CompilerParams(
          dimension_semantics=("arbitrary", "arbitrary"),
          vmem_limit_bytes=VMEM_LIMIT_BYTES),
      name="gdn_in_proj",
  )(xb, w, wat, convw, alog, dtb, alogt, dtbt)


def _block_diag2(y, bdmask):
  return jnp.where(bdmask, jnp.concatenate([y, y], axis=0), jnp.zeros((), BF16))


def _packed_inverse_minus_eye(a_list, row, colp, bdmask):
  same16 = jnp.right_shift(row, 4) == jnp.right_shift(colp, 4)
  same32 = jnp.right_shift(row, 5) == jnp.right_shift(colp, 5)
  bd = lambda xs: [_block_diag2(x.astype(BF16), bdmask) for x in xs]
  mm = lambda xs, ybs: [jnp.dot(x.astype(BF16), yb, preferred_element_type=F32)
                        for x, yb in zip(xs, ybs)]
  p = [jnp.where(same16, -a, 0.0) for a in a_list]
  s = p
  s_bd = bd(s)
  for _ in range(3):
    s = mm(s, s_bd)
    s_bd = bd(s)
    ps = mm(p, s_bd)
    p = [x + y + z for x, y, z in zip(p, s, ps)]
  for mask in (same32 & jnp.logical_not(same16), jnp.logical_not(same32)):
    e = [jnp.where(mask, a, 0.0) for a in a_list]
    pe = mm(p, bd(e))
    f = [x + y for x, y in zip(e, pe)]
    fp = mm(f, bd(p))
    p = [x - y - z for x, y, z in zip(p, f, fp)]
  return p


def _split_rows(x, low_half):
  return jnp.concatenate([jnp.where(low_half, x, 0.0), jnp.where(low_half, 0.0, x)], axis=0)


def _gdn_rec_kernel(q_ref, k_ref, v_ref, gate_ref, beta_ref, gcc_ref, gcr_ref,
                    nw_ref, o_ref, s_ref, uw_ref, qk_ref):
  @pl.when(pl.program_id(1) == 0)
  def _():
    s_ref[...] = jnp.zeros(s_ref.shape, F32)

  n_chunks = REC_ROWS // CHUNK
  heads = range(GDN_HEADS)
  nw = nw_ref[...]
  row = lax.broadcasted_iota(jnp.int32, (CHUNK, V7X_LANES), 0)
  lane = lax.broadcasted_iota(jnp.int32, (CHUNK, V7X_LANES), 1)
  colp = lane & (CHUNK - 1)
  low_half = lane < CHUNK
  incl = row >= colp
  strict = row > colp
  bd_r = lax.broadcasted_iota(jnp.int32, (2 * CHUNK, V7X_LANES), 0)
  bd_l = lax.broadcasted_iota(jnp.int32, (2 * CHUNK, V7X_LANES), 1)
  bdmask = (bd_r < CHUNK) == (bd_l < CHUNK)
  kbd_r = lax.broadcasted_iota(jnp.int32, (2 * CHUNK, 2 * GDN_DK), 0)
  kbd_l = lax.broadcasted_iota(jnp.int32, (2 * CHUNK, 2 * GDN_DK), 1)
  kbdmask = (kbd_r < CHUNK) == (kbd_l < GDN_DK)
  first_head_k = lax.broadcasted_iota(jnp.int32, (CHUNK, 2 * GDN_DK), 1) < GDN_DK

  rows_of = lambda c: slice(c * CHUNK, (c + 1) * CHUNK)
  klanes = lambda h: slice(h * GDN_DK, (h + 1) * GDN_DK)
  vlanes = lambda h: slice(h * GDN_DV, (h + 1) * GDN_DV)
  col_of = lambda ref, c, h: ref[0, rows_of(c), h:h + 1]

  probs = [(c, p) for c in range(n_chunks) for p in range(GDN_PAIRS)]
  kb_list, a_list = [], []
  for c, p in probs:
    h0, h1 = 2 * p, 2 * p + 1
    pl2 = slice(h0 * GDN_DK, (h1 + 1) * GDN_DK)
    kp = k_ref[0, rows_of(c), pl2]
    beta_k = jnp.where(first_head_k, col_of(beta_ref, c, h0), col_of(beta_ref, c, h1))
    kb = kp.astype(F32) * beta_k
    lhs = jnp.concatenate([kb.astype(BF16), q_ref[0, rows_of(c), pl2]], axis=0)
    kbd = jnp.where(kbdmask, jnp.concatenate([kp, kp], axis=0), jnp.zeros((), BF16))
    aq = lax.dot_general(lhs, kbd, (((1,), (1,)), ((), ())), preferred_element_type=F32)
    gccp = jnp.where(low_half, col_of(gcc_ref, c, h0), col_of(gcc_ref, c, h1))
    gcrp = gcr_ref[0, c, p:p + 1, :]
    decay = jnp.where(incl, jnp.exp(jnp.where(incl, gccp - gcrp, 0.0)), 0.0)
    a_list.append(jnp.where(strict, aq[:CHUNK] * decay, 0.0))
    qk_ref[c, p] = aq[CHUNK:] * decay
    kb_list.append(kb)
  c_list = _packed_inverse_minus_eye(a_list, row, colp, bdmask)
  for (c, p), kb, cm in zip(probs, kb_list, c_list):
    rhs = []
    for j, h in enumerate((2 * p, 2 * p + 1)):
      beta_h = col_of(beta_ref, c, h)
      eg_h = jnp.exp(col_of(gcc_ref, c, h))
      rhs.append(jnp.concatenate(
          [v_ref[0, rows_of(c), vlanes(h)].astype(F32) * beta_h,
           kb[:, j * GDN_DK:(j + 1) * GDN_DK] * eg_h], axis=1))
    rhs = jnp.concatenate(rhs, axis=0)
    uw = rhs + _mm(_split_rows(cm, low_half), rhs)
    uw_ref[c, 2 * p] = uw[:CHUNK]
    uw_ref[c, 2 * p + 1] = uw[CHUNK:]

  for c in range(n_chunks):
    gcc = [col_of(gcc_ref, c, h) for h in heads]
    g_last = [x[CHUNK - 1:CHUNK, :] for x in gcc]
    r = []
    for h in heads:
      qd = q_ref[0, rows_of(c), klanes(h)].astype(F32) * jnp.exp(gcc[h])
      r.append(_mm(jnp.concatenate([uw_ref[c, h, :, GDN_DV:], qd], axis=0), s_ref[h]))
    v_new = [uw_ref[c, h, :, :GDN_DV] - r[h][:CHUNK] for h in heads]
    o = [None] * GDN_HEADS
    for p in range(GDN_PAIRS):
      h0, h1 = 2 * p, 2 * p + 1
      res = _mm(_split_rows(qk_ref[c, p], low_half),
                jnp.concatenate([v_new[h0], v_new[h1]], axis=0))
      o[h0] = r[h0][CHUNK:] + res[:CHUNK]
      o[h1] = r[h1][CHUNK:] + res[CHUNK:]
    for h in heads:
      k_dec = k_ref[0, rows_of(c), klanes(h)].astype(F32) * jnp.exp(g_last[h] - gcc[h])
      s_ref[h] = s_ref[h] * jnp.exp(g_last[h]) + _mm_tn(k_dec, v_new[h])
    for h in heads:
      gate = gate_ref[0, rows_of(c), vlanes(h)].astype(F32)
      o_ref[0, rows_of(c), vlanes(h)] = _gated_rmsnorm(o[h], gate, nw).astype(BF16)


def _gdn_recurrence(q, k, v, gate, beta, gcc, gcr, norm_w):
  B, T, _ = q.shape
  tt = REC_ROWS
  H = GDN_HEADS
  rowmap = lambda b, t: (b, t, 0)
  return pl.pallas_call(
      _gdn_rec_kernel,
      grid=(B, T // tt),
      in_specs=[
          pl.BlockSpec((1, tt, GDN_KEY), rowmap),
          pl.BlockSpec((1, tt, GDN_KEY), rowmap),
          pl.BlockSpec((1, tt, GDN_VAL), rowmap),
          pl.BlockSpec((1, tt, GDN_VAL), rowmap),
          pl.BlockSpec((1, tt, H), rowmap),
          pl.BlockSpec((1, tt, H), rowmap),
          pl.BlockSpec((1, tt // CHUNK, GDN_PAIRS, V7X_LANES), lambda b, t: (b, t, 0, 0)),
          pl.BlockSpec((1, GDN_DV), lambda b, t: (0, 0)),
      ],
      out_specs=pl.BlockSpec((1, tt, GDN_VAL), rowmap),
      out_shape=jax.ShapeDtypeStruct((B, T, GDN_VAL), BF16),
      scratch_shapes=[
          pltpu.VMEM((H, GDN_DK, GDN_DV), F32),
          pltpu.VMEM((tt // CHUNK, H, CHUNK, GDN_DV + GDN_DK), F32),
          pltpu.VMEM((tt // CHUNK, GDN_PAIRS, CHUNK, V7X_LANES), F32),
      ],
      compiler_params=pltpu.CompilerParams(
          dimension_semantics=("arbitrary", "arbitrary"),
          vmem_limit_bytes=VMEM_LIMIT_BYTES),
      name="gdn_recurrence",
  )(q, k, v, gate, beta, gcc, gcr, norm_w.reshape(1, GDN_DV))


def _gla_in_kernel(x_ref, w_ref, wup_ref, bgk_ref, q_ref, k_ref, v_ref, gate_ref, b_ref):
  tm = IN_ROWS
  xb = x_ref[0]
  for c in range((GLA_QKV + GLA_VAL) // COL_TILE):
    cs = slice(c * COL_TILE, (c + 1) * COL_TILE)
    acc = jnp.dot(xb, w_ref[:, cs], preferred_element_type=F32)
    col0 = c * COL_TILE
    if col0 < GLA_KEY:
      q_ref[0, :, cs] = (acc * (GLA_DK ** -0.5)).astype(BF16)
    elif col0 < 2 * GLA_KEY:
      k_ref[0, :, col0 - GLA_KEY:col0 - GLA_KEY + COL_TILE] = acc.astype(BF16)
    elif col0 < GLA_QKV:
      v_ref[0, :, col0 - 2 * GLA_KEY:col0 - 2 * GLA_KEY + COL_TILE] = acc.astype(BF16)
    else:
      gate_ref[0, :, col0 - GLA_QKV:col0 - GLA_QKV + COL_TILE] = acc.astype(BF16)

  low = jnp.dot(xb, w_ref[:, GLA_QKV + GLA_VAL:], preferred_element_type=F32)
  logit = _mm(low, wup_ref[...]) + bgk_ref[...]
  gk = (jnp.minimum(logit, 0.0) - jnp.log1p(jnp.exp(-jnp.abs(logit)))) * (1.0 / GLA_GATE_NORM)
  lmat = _chunk_tri(CHUNK, lower=True)
  for c in range(tm // CHUNK):
    rs = slice(c * CHUNK, (c + 1) * CHUNK)
    acc = jnp.zeros((CHUNK, GLA_KEY), F32)
    for part in _split3(gk[rs]):
      acc = acc + jnp.dot(lmat, part, preferred_element_type=F32)
    b_ref[0, rs, :] = acc


def _gla_in_proj(xb, w, wup, bgk):
  B, T, D = xb.shape
  tm = IN_ROWS
  const = lambda b, t: (0, 0)
  rowmap = lambda b, t: (b, t, 0)
  return pl.pallas_call(
      _gla_in_kernel,
      grid=(B, T // tm),
      in_specs=[
          pl.BlockSpec((1, tm, D), rowmap),
          pl.BlockSpec(w.shape, const),
          pl.BlockSpec((GLA_RANK, GLA_KEY), const),
          pl.BlockSpec((1, GLA_KEY), const),
      ],
      out_specs=[
          pl.BlockSpec((1, tm, GLA_KEY), rowmap),
          pl.BlockSpec((1, tm, GLA_KEY), rowmap),
          pl.BlockSpec((1, tm, GLA_VAL), rowmap),
          pl.BlockSpec((1, tm, GLA_VAL), rowmap),
          pl.BlockSpec((1, tm, GLA_KEY), rowmap),
      ],
      out_shape=[
          jax.ShapeDtypeStruct((B, T, GLA_KEY), BF16),
          jax.ShapeDtypeStruct((B, T, GLA_KEY), BF16),
          jax.ShapeDtypeStruct((B, T, GLA_VAL), BF16),
          jax.ShapeDtypeStruct((B, T, GLA_VAL), BF16),
          jax.ShapeDtypeStruct((B, T, GLA_KEY), F32),
      ],
      compiler_params=pltpu.CompilerParams(
          dimension_semantics=("arbitrary", "arbitrary"),
          vmem_limit_bytes=VMEM_LIMIT_BYTES),
      name="gla_in_proj",
  )(xb, w, wup, bgk)


def _gla_chunk_all_heads(c, q_ref, k_ref, v_ref, gate_ref, b_ref, nw, o_ref, st_ref):
  heads = range(GLA_HEADS)
  nsub = CHUNK // SUB
  half = V7X_SUBLANES
  r0c = pl.multiple_of(c * CHUNK, CHUNK)
  rows = pl.ds(r0c, CHUNK)
  ks = [slice(h * GLA_DK, (h + 1) * GLA_DK) for h in heads]
  vs = [slice(h * GLA_DV, (h + 1) * GLA_DV) for h in heads]
  q = [q_ref[0, rows, ks[h]].astype(F32) for h in heads]
  k = [k_ref[0, rows, ks[h]].astype(F32) for h in heads]
  b2 = [b_ref[0, rows, ks[h]] * LOG2E for h in heads]
  b2_last = [x[CHUNK - 1:CHUNK, :] for x in b2]
  o_inter = [_mm_nt(q[h] * jnp.exp2(b2[h]), st_ref[h]) for h in heads]

  rows8 = lax.broadcasted_iota(jnp.int32, (half, 1), 0)
  rows64 = lax.broadcasted_iota(jnp.int32, (CHUNK, 1), 0)
  lane64 = lax.broadcasted_iota(jnp.int32, (half, CHUNK), 1)
  bands = [[None] * nsub for _ in heads]
  for h in heads:
    for i in range(nsub):
      r0 = i * SUB
      q_lo, q_hi = q[h][r0:r0 + half], q[h][r0 + half:r0 + SUB]
      b_lo, b_hi = b2[h][r0:r0 + half], b2[h][r0 + half:r0 + SUB]
      band_lo = jnp.zeros((half, CHUNK), F32)
      band_hi = jnp.zeros((half, CHUNK), F32)
      for s in range(SUB):
        b_s = b2[h][r0 + s:r0 + s + 1]
        k_s = k[h][r0 + s:r0 + s + 1]
        if s < half:
          d = b_lo - b_s
          if s > 0:
            d = d + jnp.where(rows8 >= s, 0.0, NEG_BIG)
          col = jnp.sum(q_lo * k_s * jnp.exp2(d), axis=-1, keepdims=True)
          band_lo = jnp.where(lane64 == r0 + s, col, band_lo)
          d = b_hi - b_s
        else:
          d = b_hi - b_s
          if s > half:
            d = d + jnp.where(rows8 >= s - half, 0.0, NEG_BIG)
        col = jnp.sum(q_hi * k_s * jnp.exp2(d), axis=-1, keepdims=True)
        band_hi = jnp.where(lane64 == r0 + s, col, band_hi)
      bands[h][i] = jnp.concatenate([band_lo, band_hi], axis=0)
  for i in range(1, nsub):
    r0 = i * SUB
    earlier = rows64 < r0
    for h in heads:
      ref = b2[h][r0:r0 + 1]
      qh = q[h][r0:r0 + SUB] * jnp.exp2(b2[h][r0:r0 + SUB] - ref)
      kh = k[h] * jnp.exp2(jnp.where(earlier, ref - b2[h], NEG_BIG))
      bands[h][i] = bands[h][i] + _mm_nt(qh, kh)
  o = [o_inter[h] + _mm(jnp.concatenate(bands[h], axis=0), v_ref[0, rows, vs[h]])
       for h in heads]
  for h in heads:
    st_ref[h] = (st_ref[h] * jnp.exp2(b2_last[h])
                 + _mm_tn(v_ref[0, rows, vs[h]], k[h] * jnp.exp2(b2_last[h] - b2[h])))
  for h in heads:
    gate = gate_ref[0, rows, vs[h]].astype(F32)
    o_ref[0, rows, vs[h]] = _gated_rmsnorm(o[h], gate, nw).astype(BF16)


def _gla_rec_kernel(q_ref, k_ref, v_ref, gate_ref, b_ref, nw_ref, o_ref, st_ref):
  @pl.when(pl.program_id(1) == 0)
  def _():
    st_ref[...] = jnp.zeros(st_ref.shape, F32)

  nw = nw_ref[...]

  def body(c, carry):
    _gla_chunk_all_heads(c, q_ref, k_ref, v_ref, gate_ref, b_ref, nw, o_ref, st_ref)
    return carry

  lax.fori_loop(0, REC_ROWS // CHUNK, body, 0)


def _gla_recurrence(q, k, v, gate, b, norm_w):
  B, T, _ = q.shape
  tt = REC_ROWS
  rowmap = lambda bb, t: (bb, t, 0)
  return pl.pallas_call(
      _gla_rec_kernel,
      grid=(B, T // tt),
      in_specs=[
          pl.BlockSpec((1, tt, GLA_KEY), rowmap),
          pl.BlockSpec((1, tt, GLA_KEY), rowmap),
          pl.BlockSpec((1, tt, GLA_VAL), rowmap),
          pl.BlockSpec((1, tt, GLA_VAL), rowmap),
          pl.BlockSpec((1, tt, GLA_KEY), rowmap),
          pl.BlockSpec((1, GLA_DV), lambda bb, t: (0, 0)),
      ],
      out_specs=pl.BlockSpec((1, tt, GLA_VAL), rowmap),
      out_shape=jax.ShapeDtypeStruct((B, T, GLA_VAL), BF16),
      scratch_shapes=[pltpu.VMEM((GLA_HEADS, GLA_DV, GLA_DK), F32)],
      compiler_params=pltpu.CompilerParams(
          dimension_semantics=("arbitrary", "arbitrary"),
          vmem_limit_bytes=VMEM_LIMIT_BYTES),
      name="gla_recurrence",
  )(q, k, v, gate, b, norm_w.reshape(1, GLA_DV))


def _out_kernel(o_ref, x_ref, w_ref, g_ref, b_ref, y_ref, *maybe_yb_ref):
  y = jnp.dot(o_ref[...], w_ref[...], preferred_element_type=F32)
  z = DEEP_ALPHA * x_ref[...] + y
  mu = jnp.mean(z, axis=-1, keepdims=True)
  zc = z - mu
  var = jnp.mean(zc * zc, axis=-1, keepdims=True)
  out = zc * lax.rsqrt(var + LN_EPS) * g_ref[...] + b_ref[...]
  y_ref[...] = out
  for yb_ref in maybe_yb_ref:
    yb_ref[...] = out.astype(BF16)


def _out_proj_norm(o, x, w_out, ln_g, ln_b, with_bf16_copy):
  n, dv = o.shape
  tm = OUT_ROWS
  const = lambda i: (0, 0)
  rowmap = lambda i: (i, 0)
  n_out = 2 if with_bf16_copy else 1
  outs = pl.pallas_call(
      _out_kernel,
      grid=(n // tm,),
      in_specs=[
          pl.BlockSpec((tm, dv), rowmap),
          pl.BlockSpec((tm, D_MODEL), rowmap),
          pl.BlockSpec((dv, D_MODEL), const),
          pl.BlockSpec((1, D_MODEL), const),
          pl.BlockSpec((1, D_MODEL), const),
      ],
      out_specs=[pl.BlockSpec((tm, D_MODEL), rowmap)] * n_out,
      out_shape=[jax.ShapeDtypeStruct((n, D_MODEL), F32),
                 jax.ShapeDtypeStruct((n, D_MODEL), BF16)][:n_out],
      compiler_params=pltpu.CompilerParams(
          dimension_semantics=("arbitrary",),
          vmem_limit_bytes=VMEM_LIMIT_BYTES),
      name="out_proj_norm",
  )(o, x, w_out, ln_g.reshape(1, D_MODEL), ln_b.reshape(1, D_MODEL))
  return outs[0], (outs[1] if with_bf16_copy else None)


def _gdn_mixer(xb, w_in, conv_w, a_log, dt_bias, norm_w):
  H = GDN_HEADS
  order = jnp.array(list(range(0, H, 2)) + list(range(1, H, 2)), jnp.int32)
  wat = w_in[:, GDN_QKV + GDN_VAL + H:].T[order].astype(BF16)
  q, k, v, gate, beta, gcc, gcr = _gdn_in_proj(
      xb, w_in.astype(BF16), wat, conv_w, a_log.reshape(1, H), dt_bias.reshape(1, H),
      a_log[order].reshape(H, 1), dt_bias[order].reshape(H, 1))
  return _gdn_recurrence(q, k, v, gate, beta, gcc, gcr, norm_w)


def _gla_mixer(xb, w_in, w_gk_up, b_gk, norm_w):
  q, k, v, gate, b = _gla_in_proj(xb, w_in.astype(BF16), w_gk_up.astype(BF16),
                                  b_gk.reshape(1, GLA_KEY))
  return _gla_recurrence(q, k, v, gate, b, norm_w)


def kernel(x, gdn_w_in, gdn_conv_w, gdn_a_log, gdn_dt_bias, gdn_norm_w, gdn_w_out,
           gla_w_in, gla_w_gk_up, gla_b_gk, gla_norm_w, gla_w_out, ln_g, ln_b):
  B, T, D = x.shape
  xb = x.astype(BF16)
  x = x.reshape(B * T, D)
  for i in range(DEPTH):
    j = i // 2
    if i % 2 == 0:
      o = _gdn_mixer(xb, gdn_w_in[j], gdn_conv_w[j], gdn_a_log[j], gdn_dt_bias[j],
                     gdn_norm_w[j])
      w_out = gdn_w_out[j]
    else:
      o = _gla_mixer(xb, gla_w_in[j], gla_w_gk_up[j], gla_b_gk[j], gla_norm_w[j])
      w_out = gla_w_out[j]
    x, xb = _out_proj_norm(o.reshape(B * T, -1), x, w_out.astype(BF16), ln_g[i], ln_b[i],
                           with_bf16_copy=i + 1 < DEPTH)
    if xb is not None:
      xb = xb.reshape(B, T, D)
  return x.reshape(B, T, D)
```

```python
import jax
import jax.numpy as jnp
from jax import lax
from jax.experimental import pallas as pl
from jax.experimental.pallas import tpu as pltpu

F32 = jnp.float32
BF16 = jnp.bfloat16

D_MODEL = 1024
DEPTH = 4
CHUNK = 64
CHUNK_SHIFT = 6
CONV_K = 4

GDN_HEADS = 8
GDN_DK = 128
GDN_DV = 256
GDN_KEY = GDN_HEADS * GDN_DK
GDN_VAL = GDN_HEADS * GDN_DV
GDN_QKV = 2 * GDN_KEY + GDN_VAL
GDN_PAIRS = GDN_HEADS // 2

GLA_HEADS = 4
GLA_DK = 128
GLA_DV = 256
GLA_KEY = GLA_HEADS * GLA_DK
GLA_VAL = GLA_HEADS * GLA_DV
GLA_QKV = 2 * GLA_KEY + GLA_VAL
GLA_RANK = 16
GLA_GATE_NORM = 16.0

DEEP_ALPHA = (2.0 * DEPTH) ** 0.25
LN_EPS = 1e-5
RMS_EPS = 1e-6
L2_EPS = 1e-6
LOG2E = 1.4426950408889634

V7X_MXU_WIDTH = 256
V7X_SUBLANES = 8
V7X_LANES = 128
VMEM_LIMIT_BYTES = 56 * 1024 * 1024

IN_ROWS = 256
REC_ROWS = 512
OUT_ROWS = 1024
OUT_BLOCK = 256
COL_TILE = V7X_MXU_WIDTH
SUB = 16
NEG_BIG = -1e30


def _mm(a, b):
  return jnp.dot(a.astype(BF16), b.astype(BF16), preferred_element_type=F32)


def _mm_nt(a, b):
  return lax.dot_general(a.astype(BF16), b.astype(BF16),
                         (((1,), (1,)), ((), ())), preferred_element_type=F32)


def _mm_tn(a, b):
  return lax.dot_general(a.astype(BF16), b.astype(BF16),
                         (((0,), (0,)), ((), ())), preferred_element_type=F32)


def _split3(x):
  hi = x.astype(BF16)
  r1 = x - hi.astype(F32)
  mid = r1.astype(BF16)
  lo = (r1 - mid.astype(F32)).astype(BF16)
  return hi, mid, lo


def _sigmoid(x):
  return 1.0 / (1.0 + jnp.exp(-x))


def _softplus(x):
  return jnp.maximum(x, 0.0) + jnp.log1p(jnp.exp(-jnp.abs(x)))


def _silu(x):
  return x * _sigmoid(x)


def _chunk_tri(n, lower):
  r = lax.broadcasted_iota(jnp.int32, (n, n), 0)
  c = lax.broadcasted_iota(jnp.int32, (n, n), 1)
  same = jnp.right_shift(r, CHUNK_SHIFT) == jnp.right_shift(c, CHUNK_SHIFT)
  tri = (c <= r) if lower else (r <= c)
  return jnp.where(same & tri, 1.0, 0.0).astype(BF16)


def _gated_rmsnorm(o, gate, nw):
  ms = jnp.mean(o * o, axis=-1, keepdims=True)
  return o * lax.rsqrt(ms + RMS_EPS) * nw * _silu(gate)


def _gdn_in_kernel(x_ref, w_ref, wat_ref, convw_ref, alog_ref, dtb_ref, alogt_ref, dtbt_ref,
                   q_ref, k_ref, v_ref, gate_ref, beta_ref, gcc_ref, gcr_ref, pbuf):
  tm = IN_ROWS
  halo = V7X_SUBLANES
  H = GDN_HEADS

  @pl.when(pl.program_id(1) == 0)
  def _():
    pbuf[0:halo, :] = jnp.zeros((halo, GDN_QKV), F32)

  xb = x_ref[0]

  def gate_cols(c):
    cs = slice(c * COL_TILE, (c + 1) * COL_TILE)
    ws = slice(GDN_QKV + c * COL_TILE, GDN_QKV + (c + 1) * COL_TILE)
    gate_ref[0, :, cs] = jnp.dot(xb, w_ref[:, ws], preferred_element_type=F32).astype(BF16)

  n_cols = GDN_QKV // COL_TILE
  proj = lambda c: jnp.dot(xb, w_ref[:, c * COL_TILE:(c + 1) * COL_TILE],
                           preferred_element_type=F32)
  acc_next = proj(0)
  for c in range(n_cols):
    cs = slice(c * COL_TILE, (c + 1) * COL_TILE)
    acc = acc_next
    pbuf[halo:halo + tm, cs] = acc
    if c + 1 < n_cols:
      acc_next = proj(c + 1)
    cw = convw_ref[:, cs]
    y = acc * cw[CONV_K - 1:CONV_K]
    for j in range(CONV_K - 1):
      off = halo - (CONV_K - 1) + j
      y = y + pbuf[off:off + tm, cs] * cw[j:j + 1]
    y = _silu(y)
    col0 = c * COL_TILE
    if col0 < 2 * GDN_KEY:
      parts = []
      for j in range(COL_TILE // GDN_DK):
        t = y[:, j * GDN_DK:(j + 1) * GDN_DK]
        parts.append(t * lax.rsqrt(jnp.sum(t * t, axis=-1, keepdims=True) + L2_EPS))
      y = jnp.concatenate(parts, axis=1)
      if col0 < GDN_KEY:
        q_ref[0, :, cs] = (y * (GDN_DK ** -0.5)).astype(BF16)
      else:
        k_ref[0, :, col0 - GDN_KEY:col0 - GDN_KEY + COL_TILE] = y.astype(BF16)
    else:
      v_ref[0, :, col0 - 2 * GDN_KEY:col0 - 2 * GDN_KEY + COL_TILE] = y.astype(BF16)
    if c % 2 == 1:
      gate_cols(c // 2)
  pbuf[0:halo, :] = pbuf[tm:tm + halo, :]

  ba = jnp.dot(xb, w_ref[:, GDN_QKV + GDN_VAL:], preferred_element_type=F32)
  beta_ref[0] = _sigmoid(ba[:, :H])
  g = -jnp.exp(alog_ref[...]) * _softplus(ba[:, H:] + dtb_ref[...])
  lmat = _chunk_tri(tm, lower=True)
  gc = jnp.zeros((tm, H), F32)
  for part in _split3(g):
    gc = gc + jnp.dot(lmat, part, preferred_element_type=F32)
  gcc_ref[0] = gc

  at = lax.dot_general(wat_ref[...], xb, (((1,), (1,)), ((), ())),
                       preferred_element_type=F32)
  gt = -jnp.exp(alogt_ref[...]) * _softplus(at + dtbt_ref[...])
  umat = _chunk_tri(tm, lower=False)
  gct = jnp.zeros((H, tm), F32)
  for part in _split3(gt):
    gct = gct + jnp.dot(part, umat, preferred_element_type=F32)
  even = gct[:GDN_PAIRS]
  odd = gct[GDN_PAIRS:]
  odd_r = pltpu.roll(gct, CHUNK, 1)[GDN_PAIRS:]
  even_r = pltpu.roll(gct, tm - CHUNK, 1)[:GDN_PAIRS]
  low_half = (lax.broadcasted_iota(jnp.int32, (GDN_PAIRS, tm), 1) & (V7X_LANES - 1)) < CHUNK
  p_even = jnp.where(low_half, even, odd_r)
  p_odd = jnp.where(low_half, even_r, odd)
  for j in range(tm // V7X_LANES):
    ls = slice(j * V7X_LANES, (j + 1) * V7X_LANES)
    gcr_ref[0, 2 * j] = p_even[:, ls]
    gcr_ref[0, 2 * j + 1] = p_odd[:, ls]


def _gdn_in_proj(xb, w, wat, convw, alog, dtb, alogt, dtbt):
  B, T, D = xb.shape
  tm = IN_ROWS
  H = GDN_HEADS
  const = lambda b, t: (0, 0)
  rowmap = lambda b, t: (b, t, 0)
  return pl.pallas_call(
      _gdn_in_kernel,
      grid=(B, T // tm),
      in_specs=[
          pl.BlockSpec((1, tm, D), rowmap),
          pl.BlockSpec(w.shape, const),
          pl.BlockSpec((H, D), const),
          pl.BlockSpec((CONV_K, GDN_QKV), const),
          pl.BlockSpec((1, H), const),
          pl.BlockSpec((1, H), const),
          pl.BlockSpec((H, 1), const),
          pl.BlockSpec((H, 1), const),
      ],
      out_specs=[
          pl.BlockSpec((1, tm, GDN_KEY), rowmap),
          pl.BlockSpec((1, tm, GDN_KEY), rowmap),
          pl.BlockSpec((1, tm, GDN_VAL), rowmap),
          pl.BlockSpec((1, tm, GDN_VAL), rowmap),
          pl.BlockSpec((1, tm, H), rowmap),
          pl.BlockSpec((1, tm, H), rowmap),
          pl.BlockSpec((1, tm // CHUNK, GDN_PAIRS, V7X_LANES), lambda b, t: (b, t, 0, 0)),
      ],
      out_shape=[
          jax.ShapeDtypeStruct((B, T, GDN_KEY), BF16),
          jax.ShapeDtypeStruct((B, T, GDN_KEY), BF16),
          jax.ShapeDtypeStruct((B, T, GDN_VAL), BF16),
          jax.ShapeDtypeStruct((B, T, GDN_VAL), BF16),
          jax.ShapeDtypeStruct((B, T, H), F32),
          jax.ShapeDtypeStruct((B, T, H), F32),
          jax.ShapeDtypeStruct((B, T // CHUNK, GDN_PAIRS, V7X_LANES), F32),
      ],
      scratch_shapes=[pltpu.VMEM((tm + V7X_SUBLANES, GDN_QKV), F32)],
      compiler_params=pltpu.CompilerParams(
          dimension_semantics=("arbitrary", "arbitrary"),
          vmem_limit_bytes=VMEM_LIMIT_BYTES),
      name="gdn_in_proj",
  )(xb, w, wat, convw, alog, dtb, alogt, dtbt)


def _block_diag2(y, bdmask):
  return jnp.where(bdmask, jnp.concatenate([y, y], axis=0), jnp.zeros((), BF16))


def _packed_inverse_minus_eye(a_list, row, colp, bdmask):
  same16 = jnp.right_shift(row, 4) == jnp.right_shift(colp, 4)
  same32 = jnp.right_shift(row, 5) == jnp.right_shift(colp, 5)
  bd = lambda xs: [_block_diag2(x.astype(BF16), bdmask) for x in xs]
  mm = lambda xs, ybs: [jnp.dot(x.astype(BF16), yb, preferred_element_type=F32)
                        for x, yb in zip(xs, ybs)]
  p = [jnp.where(same16, -a, 0.0) for a in a_list]
  s = p
  s_bd = bd(s)
  for _ in range(3):
    s = mm(s, s_bd)
    s_bd = bd(s)
    ps = mm(p, s_bd)
    p = [x + y + z for x, y, z in zip(p, s, ps)]
  for mask in (same32 & jnp.logical_not(same16), jnp.logical_not(same32)):
    e = [jnp.where(mask, a, 0.0) for a in a_list]
    pe = mm(p, bd(e))
    f = [x + y for x, y in zip(e, pe)]
    fp = mm(f, bd(p))
    p = [x - y - z for x, y, z in zip(p, f, fp)]
  return p


def _split_rows(x, low_half):
  return jnp.concatenate([jnp.where(low_half, x, 0.0), jnp.where(low_half, 0.0, x)], axis=0)


def _gdn_rec_kernel(q_ref, k_ref, v_ref, gate_ref, beta_ref, gcc_ref, gcr_ref,
                    nw_ref, o_ref, s_ref, uw_ref, qk_ref):
  @pl.when(pl.program_id(1) == 0)
  def _():
    s_ref[...] = jnp.zeros(s_ref.shape, F32)

  n_chunks = REC_ROWS // CHUNK
  heads = range(GDN_HEADS)
  nw = nw_ref[...]
  row = lax.broadcasted_iota(jnp.int32, (CHUNK, V7X_LANES), 0)
  lane = lax.broadcasted_iota(jnp.int32, (CHUNK, V7X_LANES), 1)
  colp = lane & (CHUNK - 1)
  low_half = lane < CHUNK
  incl = row >= colp
  strict = row > colp
  bd_r = lax.broadcasted_iota(jnp.int32, (2 * CHUNK, V7X_LANES), 0)
  bd_l = lax.broadcasted_iota(jnp.int32, (2 * CHUNK, V7X_LANES), 1)
  bdmask = (bd_r < CHUNK) == (bd_l < CHUNK)
  kbd_r = lax.broadcasted_iota(jnp.int32, (2 * CHUNK, 2 * GDN_DK), 0)
  kbd_l = lax.broadcasted_iota(jnp.int32, (2 * CHUNK, 2 * GDN_DK), 1)
  kbdmask = (kbd_r < CHUNK) == (kbd_l < GDN_DK)
  first_head_k = lax.broadcasted_iota(jnp.int32, (CHUNK, 2 * GDN_DK), 1) < GDN_DK

  rows_of = lambda c: slice(c * CHUNK, (c + 1) * CHUNK)
  klanes = lambda h: slice(h * GDN_DK, (h + 1) * GDN_DK)
  vlanes = lambda h: slice(h * GDN_DV, (h + 1) * GDN_DV)
  col_of = lambda ref, c, h: ref[0, rows_of(c), h:h + 1]

  probs = [(c, p) for c in range(n_chunks) for p in range(GDN_PAIRS)]
  kb_list, a_list = [], []
  for c, p in probs:
    h0, h1 = 2 * p, 2 * p + 1
    pl2 = slice(h0 * GDN_DK, (h1 + 1) * GDN_DK)
    kp = k_ref[0, rows_of(c), pl2]
    beta_k = jnp.where(first_head_k, col_of(beta_ref, c, h0), col_of(beta_ref, c, h1))
    kb = kp.astype(F32) * beta_k
    lhs = jnp.concatenate([kb.astype(BF16), q_ref[0, rows_of(c), pl2]], axis=0)
    kbd = jnp.where(kbdmask, jnp.concatenate([kp, kp], axis=0), jnp.zeros((), BF16))
    aq = lax.dot_general(lhs, kbd, (((1,), (1,)), ((), ())), preferred_element_type=F32)
    gccp = jnp.where(low_half, col_of(gcc_ref, c, h0), col_of(gcc_ref, c, h1))
    gcrp = gcr_ref[0, c, p:p + 1, :]
    decay = jnp.where(incl, jnp.exp(jnp.where(incl, gccp - gcrp, 0.0)), 0.0)
    a_list.append(jnp.where(strict, aq[:CHUNK] * decay, 0.0))
    qk_ref[c, p] = aq[CHUNK:] * decay
    kb_list.append(kb)
  c_list = _packed_inverse_minus_eye(a_list, row, colp, bdmask)
  for (c, p), kb, cm in zip(probs, kb_list, c_list):
    rhs = []
    for j, h in enumerate((2 * p, 2 * p + 1)):
      beta_h = col_of(beta_ref, c, h)
      eg_h = jnp.exp(col_of(gcc_ref, c, h))
      rhs.append(jnp.concatenate(
          [v_ref[0, rows_of(c), vlanes(h)].astype(F32) * beta_h,
           kb[:, j * GDN_DK:(j + 1) * GDN_DK] * eg_h], axis=1))
    rhs = jnp.concatenate(rhs, axis=0)
    uw = rhs + _mm(_split_rows(cm, low_half), rhs)
    uw_ref[c, 2 * p] = uw[:CHUNK]
    uw_ref[c, 2 * p + 1] = uw[CHUNK:]

  for c in range(n_chunks):
    gcc = [col_of(gcc_ref, c, h) for h in heads]
    g_last = [x[CHUNK - 1:CHUNK, :] for x in gcc]
    r = []
    for h in heads:
      qd = q_ref[0, rows_of(c), klanes(h)].astype(F32) * jnp.exp(gcc[h])
      r.append(_mm(jnp.concatenate([uw_ref[c, h, :, GDN_DV:], qd], axis=0), s_ref[h]))
    v_new = [uw_ref[c, h, :, :GDN_DV] - r[h][:CHUNK] for h in heads]
    o = [None] * GDN_HEADS
    for p in range(GDN_PAIRS):
      h0, h1 = 2 * p, 2 * p + 1
      res = _mm(_split_rows(qk_ref[c, p], low_half),
                jnp.concatenate([v_new[h0], v_new[h1]], axis=0))
      o[h0] = r[h0][CHUNK:] + res[:CHUNK]
      o[h1] = r[h1][CHUNK:] + res[CHUNK:]
    for h in heads:
      k_dec = k_ref[0, rows_of(c), klanes(h)].astype(F32) * jnp.exp(g_last[h] - gcc[h])
      s_ref[h] = s_ref[h] * jnp.exp(g_last[h]) + _mm_tn(k_dec, v_new[h])
    for h in heads:
      gate = gate_ref[0, rows_of(c), vlanes(h)].astype(F32)
      o_ref[0, rows_of(c), vlanes(h)] = _gated_rmsnorm(o[h], gate, nw).astype(BF16)


def _gdn_recurrence(q, k, v, gate, beta, gcc, gcr, norm_w):
  B, T, _ = q.shape
  tt = REC_ROWS
  H = GDN_HEADS
  rowmap = lambda b, t: (b, t, 0)
  return pl.pallas_call(
      _gdn_rec_kernel,
      grid=(B, T // tt),
      in_specs=[
          pl.BlockSpec((1, tt, GDN_KEY), rowmap),
          pl.BlockSpec((1, tt, GDN_KEY), rowmap),
          pl.BlockSpec((1, tt, GDN_VAL), rowmap),
          pl.BlockSpec((1, tt, GDN_VAL), rowmap),
          pl.BlockSpec((1, tt, H), rowmap),
          pl.BlockSpec((1, tt, H), rowmap),
          pl.BlockSpec((1, tt // CHUNK, GDN_PAIRS, V7X_LANES), lambda b, t: (b, t, 0, 0)),
          pl.BlockSpec((1, GDN_DV), lambda b, t: (0, 0)),
      ],
      out_specs=pl.BlockSpec((1, tt, GDN_VAL), rowmap),
      out_shape=jax.ShapeDtypeStruct((B, T, GDN_VAL), BF16),
      scratch_shapes=[
          pltpu.VMEM((H, GDN_DK, GDN_DV), F32),
          pltpu.VMEM((tt // CHUNK, H, CHUNK, GDN_DV + GDN_DK), F32),
          pltpu.VMEM((tt // CHUNK, GDN_PAIRS, CHUNK, V7X_LANES), F32),
      ],
      compiler_params=pltpu.CompilerParams(
          dimension_semantics=("arbitrary", "arbitrary"),
          vmem_limit_bytes=VMEM_LIMIT_BYTES),
      name="gdn_recurrence",
  )(q, k, v, gate, beta, gcc, gcr, norm_w.reshape(1, GDN_DV))


def _gla_in_kernel(x_ref, w_ref, wup_ref, bgk_ref, q_ref, k_ref, v_ref, gate_ref, b_ref):
  tm = IN_ROWS
  xb = x_ref[0]

  low = jnp.dot(xb, w_ref[:, GLA_QKV + GLA_VAL:], preferred_element_type=F32)
  logit = _mm(low, wup_ref[...]) + bgk_ref[...]
  gk = (jnp.minimum(logit, 0.0) - jnp.log1p(jnp.exp(-jnp.abs(logit)))) * (1.0 / GLA_GATE_NORM)

  q_ref[0] = (jnp.dot(xb, w_ref[:, :GLA_KEY], preferred_element_type=F32)
              * (GLA_DK ** -0.5)).astype(BF16)
  k_ref[0] = jnp.dot(xb, w_ref[:, GLA_KEY:2 * GLA_KEY], preferred_element_type=F32).astype(BF16)

  lmat = _chunk_tri(CHUNK, lower=True)
  for c in range(tm // CHUNK):
    rs = slice(c * CHUNK, (c + 1) * CHUNK)
    acc = jnp.zeros((CHUNK, GLA_KEY), F32)
    for part in _split3(gk[rs]):
      acc = acc + jnp.dot(lmat, part, preferred_element_type=F32)
    b_ref[0, rs, :] = acc

  v_ref[0] = jnp.dot(xb, w_ref[:, 2 * GLA_KEY:GLA_QKV], preferred_element_type=F32).astype(BF16)
  gate_ref[0] = jnp.dot(xb, w_ref[:, GLA_QKV:GLA_QKV + GLA_VAL],
                        preferred_element_type=F32).astype(BF16)


def _gla_in_proj(xb, w, wup, bgk):
  B, T, D = xb.shape
  tm = IN_ROWS
  const = lambda b, t: (0, 0)
  rowmap = lambda b, t: (b, t, 0)
  return pl.pallas_call(
      _gla_in_kernel,
      grid=(B, T // tm),
      in_specs=[
          pl.BlockSpec((1, tm, D), rowmap),
          pl.BlockSpec(w.shape, const),
          pl.BlockSpec((GLA_RANK, GLA_KEY), const),
          pl.BlockSpec((1, GLA_KEY), const),
      ],
      out_specs=[
          pl.BlockSpec((1, tm, GLA_KEY), rowmap),
          pl.BlockSpec((1, tm, GLA_KEY), rowmap),
          pl.BlockSpec((1, tm, GLA_VAL), rowmap),
          pl.BlockSpec((1, tm, GLA_VAL), rowmap),
          pl.BlockSpec((1, tm, GLA_KEY), rowmap),
      ],
      out_shape=[
          jax.ShapeDtypeStruct((B, T, GLA_KEY), BF16),
          jax.ShapeDtypeStruct((B, T, GLA_KEY), BF16),
          jax.ShapeDtypeStruct((B, T, GLA_VAL), BF16),
          jax.ShapeDtypeStruct((B, T, GLA_VAL), BF16),
          jax.ShapeDtypeStruct((B, T, GLA_KEY), F32),
      ],
      compiler_params=pltpu.CompilerParams(
          dimension_semantics=("arbitrary", "arbitrary"),
          vmem_limit_bytes=VMEM_LIMIT_BYTES),
      name="gla_in_proj",
  )(xb, w, wup, bgk)


def _gla_chunk_all_heads(c, q_ref, k_ref, v_ref, gate_ref, b_ref, nw, o_ref, st_ref,
                         b2_ref, kf_ref):
  heads = range(GLA_HEADS)
  nsub = CHUNK // SUB
  half = V7X_SUBLANES
  r0c = pl.multiple_of(c * CHUNK, CHUNK)
  rows = pl.ds(r0c, CHUNK)
  ks = [slice(h * GLA_DK, (h + 1) * GLA_DK) for h in heads]
  vs = [slice(h * GLA_DV, (h + 1) * GLA_DV) for h in heads]
  q = [q_ref[0, rows, ks[h]].astype(F32) for h in heads]
  k = [k_ref[0, rows, ks[h]].astype(F32) for h in heads]
  b2 = [b_ref[0, rows, ks[h]] * LOG2E for h in heads]
  b2_last = [x[CHUNK - 1:CHUNK, :] for x in b2]
  for h in heads:
    b2_ref[h] = b2[h]
    kf_ref[h] = k[h]
  o_inter = [_mm_nt(q[h] * jnp.exp2(b2[h]), st_ref[h]) for h in heads]

  rows8 = lax.broadcasted_iota(jnp.int32, (half, 1), 0)
  lane64 = lax.broadcasted_iota(jnp.int32, (half, CHUNK), 1)
  bands = [[None] * nsub for _ in heads]
  for h in heads:
    for i in range(nsub):
      r0 = i * SUB
      q_lo, q_hi = q[h][r0:r0 + half], q[h][r0 + half:r0 + SUB]
      b_lo, b_hi = b2[h][r0:r0 + half], b2[h][r0 + half:r0 + SUB]
      band_lo = jnp.zeros((half, CHUNK), F32)
      band_hi = jnp.zeros((half, CHUNK), F32)
      for s in range(SUB):
        b_s = b2_ref[h, r0 + s:r0 + s + 1, :]
        k_s = kf_ref[h, r0 + s:r0 + s + 1, :]
        if s < half:
          d = b_lo - b_s
          if s > 0:
            d = d + jnp.where(rows8 >= s, 0.0, NEG_BIG)
          col = jnp.sum(q_lo * k_s * jnp.exp2(d), axis=-1, keepdims=True)
          band_lo = jnp.where(lane64 == r0 + s, col, band_lo)
          d = b_hi - b_s
        else:
          d = b_hi - b_s
          if s > half:
            d = d + jnp.where(rows8 >= s - half, 0.0, NEG_BIG)
        col = jnp.sum(q_hi * k_s * jnp.exp2(d), axis=-1, keepdims=True)
        band_hi = jnp.where(lane64 == r0 + s, col, band_hi)
      bands[h][i] = jnp.concatenate([band_lo, band_hi], axis=0)
  for i in range(1, nsub):
    r0 = i * SUB
    for h in heads:
      ref = b2[h][r0:r0 + 1]
      qh = q[h][r0:r0 + SUB] * jnp.exp2(b2[h][r0:r0 + SUB] - ref)
      kh = jnp.concatenate([k[h][:r0] * jnp.exp2(ref - b2[h][:r0]),
                            jnp.zeros((CHUNK - r0, GLA_DK), F32)], axis=0)
      bands[h][i] = bands[h][i] + _mm_nt(qh, kh)
  o = [o_inter[h] + _mm(jnp.concatenate(bands[h], axis=0), v_ref[0, rows, vs[h]])
       for h in heads]
  for h in heads:
    st_ref[h] = (st_ref[h] * jnp.exp2(b2_last[h])
                 + _mm_tn(v_ref[0, rows, vs[h]], k[h] * jnp.exp2(b2_last[h] - b2[h])))
  for h in heads:
    gate = gate_ref[0, rows, vs[h]].astype(F32)
    o_ref[0, rows, vs[h]] = _gated_rmsnorm(o[h], gate, nw).astype(BF16)


def _gla_rec_kernel(q_ref, k_ref, v_ref, gate_ref, b_ref, nw_ref, o_ref, st_ref,
                    b2_ref, kf_ref):
  @pl.when(pl.program_id(1) == 0)
  def _():
    st_ref[...] = jnp.zeros(st_ref.shape, F32)

  nw = nw_ref[...]

  def body(c, carry):
    _gla_chunk_all_heads(c, q_ref, k_ref, v_ref, gate_ref, b_ref, nw, o_ref, st_ref,
                         b2_ref, kf_ref)
    return carry

  lax.fori_loop(0, REC_ROWS // CHUNK, body, 0)


def _gla_recurrence(q, k, v, gate, b, norm_w):
  B, T, _ = q.shape
  tt = REC_ROWS
  rowmap = lambda bb, t: (bb, t, 0)
  return pl.pallas_call(
      _gla_rec_kernel,
      grid=(B, T // tt),
      in_specs=[
          pl.BlockSpec((1, tt, GLA_KEY), rowmap),
          pl.BlockSpec((1, tt, GLA_KEY), rowmap),
          pl.BlockSpec((1, tt, GLA_VAL), rowmap),
          pl.BlockSpec((1, tt, GLA_VAL), rowmap),
          pl.BlockSpec((1, tt, GLA_KEY), rowmap),
          pl.BlockSpec((1, GLA_DV), lambda bb, t: (0, 0)),
      ],
      out_specs=pl.BlockSpec((1, tt, GLA_VAL), rowmap),
      out_shape=jax.ShapeDtypeStruct((B, T, GLA_VAL), BF16),
      scratch_shapes=[
          pltpu.VMEM((GLA_HEADS, GLA_DV, GLA_DK), F32),
          pltpu.VMEM((GLA_HEADS, CHUNK, GLA_DK), F32),
          pltpu.VMEM((GLA_HEADS, CHUNK, GLA_DK), F32),
      ],
      compiler_params=pltpu.CompilerParams(
          dimension_semantics=("arbitrary", "arbitrary"),
          vmem_limit_bytes=VMEM_LIMIT_BYTES),
      name="gla_recurrence",
  )(q, k, v, gate, b, norm_w.reshape(1, GLA_DV))


def _out_kernel(o_ref, x_ref, w_ref, g_ref, b_ref, y_ref, *maybe_yb_ref):
  nblk = OUT_ROWS // OUT_BLOCK
  rows_of = lambda i: slice(i * OUT_BLOCK, (i + 1) * OUT_BLOCK)
  proj = lambda i: jnp.dot(o_ref[rows_of(i), :], w_ref[...], preferred_element_type=F32)
  y = proj(0)
  for i in range(nblk):
    y_next = proj(i + 1) if i + 1 < nblk else None
    z = DEEP_ALPHA * x_ref[rows_of(i), :] + y
    mu = jnp.mean(z, axis=-1, keepdims=True)
    zc = z - mu
    var = jnp.mean(zc * zc, axis=-1, keepdims=True)
    out = zc * lax.rsqrt(var + LN_EPS) * g_ref[...] + b_ref[...]
    y_ref[rows_of(i), :] = out
    for yb_ref in maybe_yb_ref:
      yb_ref[rows_of(i), :] = out.astype(BF16)
    y = y_next


def _out_proj_norm(o, x, w_out, ln_g, ln_b, with_bf16_copy):
  n, dv = o.shape
  tm = OUT_ROWS
  const = lambda i: (0, 0)
  rowmap = lambda i: (i, 0)
  n_out = 2 if with_bf16_copy else 1
  outs = pl.pallas_call(
      _out_kernel,
      grid=(n // tm,),
      in_specs=[
          pl.BlockSpec((tm, dv), rowmap),
          pl.BlockSpec((tm, D_MODEL), rowmap),
          pl.BlockSpec((dv, D_MODEL), const),
          pl.BlockSpec((1, D_MODEL), const),
          pl.BlockSpec((1, D_MODEL), const),
      ],
      out_specs=[pl.BlockSpec((tm, D_MODEL), rowmap)] * n_out,
      out_shape=[jax.ShapeDtypeStruct((n, D_MODEL), F32),
                 jax.ShapeDtypeStruct((n, D_MODEL), BF16)][:n_out],
      compiler_params=pltpu.CompilerParams(
          dimension_semantics=("arbitrary",),
          vmem_limit_bytes=VMEM_LIMIT_BYTES),
      name="out_proj_norm",
  )(o, x, w_out, ln_g.reshape(1, D_MODEL), ln_b.reshape(1, D_MODEL))
  return outs[0], (outs[1] if with_bf16_copy else None)


def _gdn_mixer(xb, w_in, conv_w, a_log, dt_bias, norm_w):
  H = GDN_HEADS
  order = jnp.array(list(range(0, H, 2)) + list(range(1, H, 2)), jnp.int32)
  wat = w_in[:, GDN_QKV + GDN_VAL + H:].T[order].astype(BF16)
  q, k, v, gate, beta, gcc, gcr = _gdn_in_proj(
      xb, w_in.astype(BF16), wat, conv_w, a_log.reshape(1, H), dt_bias.reshape(1, H),
      a_log[order].reshape(H, 1), dt_bias[order].reshape(H, 1))
  return _gdn_recurrence(q, k, v, gate, beta, gcc, gcr, norm_w)


def _gla_mixer(xb, w_in, w_gk_up, b_gk, norm_w):
  q, k, v, gate, b = _gla_in_proj(xb, w_in.astype(BF16), w_gk_up.astype(BF16),
                                  b_gk.reshape(1, GLA_KEY))
  return _gla_recurrence(q, k, v, gate, b, norm_w)


def kernel(x, gdn_w_in, gdn_conv_w, gdn_a_log, gdn_dt_bias, gdn_norm_w, gdn_w_out,
           gla_w_in, gla_w_gk_up, gla_b_gk, gla_norm_w, gla_w_out, ln_g, ln_b):
  B, T, D = x.shape
  xb = x.astype(BF16)
  x = x.reshape(B * T, D)
  for i in range(DEPTH):
    j = i // 2
    if i % 2 == 0:
      o = _gdn_mixer(xb, gdn_w_in[j], gdn_conv_w[j], gdn_a_log[j], gdn_dt_bias[j],
                     gdn_norm_w[j])
      w_out = gdn_w_out[j]
    else:
      o = _gla_mixer(xb, gla_w_in[j], gla_w_gk_up[j], gla_b_gk[j], gla_norm_w[j])
      w_out = gla_w_out[j]
    x, xb = _out_proj_norm(o.reshape(B * T, -1), x, w_out.astype(BF16), ln_g[i], ln_b[i],
                           with_bf16_copy=i + 1 < DEPTH)
    if xb is not None:
      xb = xb.reshape(B, T, D)
  return x.reshape(B, T, D)
```

```python
import jax
import jax.numpy as jnp
from jax import lax
from jax.experimental import pallas as pl
from jax.experimental.pallas import tpu as pltpu

F32 = jnp.float32
BF16 = jnp.bfloat16

D_MODEL = 1024
DEPTH = 4
CHUNK = 64
CHUNK_SHIFT = 6
CONV_K = 4

GDN_HEADS = 8
GDN_DK = 128
GDN_DV = 256
GDN_KEY = GDN_HEADS * GDN_DK
GDN_VAL = GDN_HEADS * GDN_DV
GDN_QKV = 2 * GDN_KEY + GDN_VAL
GDN_PAIRS = GDN_HEADS // 2

GLA_HEADS = 4
GLA_DK = 128
GLA_DV = 256
GLA_KEY = GLA_HEADS * GLA_DK
GLA_VAL = GLA_HEADS * GLA_DV
GLA_QKV = 2 * GLA_KEY + GLA_VAL
GLA_RANK = 16
GLA_GATE_NORM = 16.0

DEEP_ALPHA = (2.0 * DEPTH) ** 0.25
LN_EPS = 1e-5
RMS_EPS = 1e-6
L2_EPS = 1e-6
LOG2E = 1.4426950408889634

V7X_MXU_WIDTH = 256
V7X_SUBLANES = 8
V7X_LANES = 128
VMEM_LIMIT_BYTES = 56 * 1024 * 1024

IN_ROWS = 256
REC_ROWS = 512
OUT_ROWS = 1024
OUT_BLOCK = 256
COL_TILE = V7X_MXU_WIDTH
SUB = 16
NEG_BIG = -1e30


def _mm(a, b):
  return jnp.dot(a.astype(BF16), b.astype(BF16), preferred_element_type=F32)


def _mm_nt(a, b):
  return lax.dot_general(a.astype(BF16), b.astype(BF16),
                         (((1,), (1,)), ((), ())), preferred_element_type=F32)


def _mm_tn(a, b):
  return lax.dot_general(a.astype(BF16), b.astype(BF16),
                         (((0,), (0,)), ((), ())), preferred_element_type=F32)


def _split3(x):
  hi = x.astype(BF16)
  r1 = x - hi.astype(F32)
  mid = r1.astype(BF16)
  lo = (r1 - mid.astype(F32)).astype(BF16)
  return hi, mid, lo


def _sigmoid(x):
  return 1.0 / (1.0 + jnp.exp(-x))


def _softplus(x):
  return jnp.maximum(x, 0.0) + jnp.log1p(jnp.exp(-jnp.abs(x)))


def _silu(x):
  return x * _sigmoid(x)


def _chunk_tri(n, lower):
  r = lax.broadcasted_iota(jnp.int32, (n, n), 0)
  c = lax.broadcasted_iota(jnp.int32, (n, n), 1)
  same = jnp.right_shift(r, CHUNK_SHIFT) == jnp.right_shift(c, CHUNK_SHIFT)
  tri = (c <= r) if lower else (r <= c)
  return jnp.where(same & tri, 1.0, 0.0).astype(BF16)


def _gated_rmsnorm(o, gate, nw):
  ms = jnp.mean(o * o, axis=-1, keepdims=True)
  return o * lax.rsqrt(ms + RMS_EPS) * nw * _silu(gate)


def _gdn_in_kernel(x_ref, w_ref, wat_ref, convw_ref, alog_ref, dtb_ref, alogt_ref, dtbt_ref,
                   q_ref, k_ref, v_ref, gate_ref, beta_ref, gcc_ref, gcr_ref, pbuf):
  tm = IN_ROWS
  halo = V7X_SUBLANES
  H = GDN_HEADS

  @pl.when(pl.program_id(1) == 0)
  def _():
    pbuf[0:halo, :] = jnp.zeros((halo, GDN_QKV), F32)

  xb = x_ref[0]

  def gate_cols(c):
    cs = slice(c * COL_TILE, (c + 1) * COL_TILE)
    ws = slice(GDN_QKV + c * COL_TILE, GDN_QKV + (c + 1) * COL_TILE)
    gate_ref[0, :, cs] = jnp.dot(xb, w_ref[:, ws], preferred_element_type=F32).astype(BF16)

  n_cols = GDN_QKV // COL_TILE
  proj = lambda c: jnp.dot(xb, w_ref[:, c * COL_TILE:(c + 1) * COL_TILE],
                           preferred_element_type=F32)
  acc_next = proj(0)
  for c in range(n_cols):
    cs = slice(c * COL_TILE, (c + 1) * COL_TILE)
    acc = acc_next
    pbuf[halo:halo + tm, cs] = acc
    if c + 1 < n_cols:
      acc_next = proj(c + 1)
    cw = convw_ref[:, cs]
    y = acc * cw[CONV_K - 1:CONV_K]
    for j in range(CONV_K - 1):
      off = halo - (CONV_K - 1) + j
      y = y + pbuf[off:off + tm, cs] * cw[j:j + 1]
    y = _silu(y)
    col0 = c * COL_TILE
    if col0 < 2 * GDN_KEY:
      parts = []
      for j in range(COL_TILE // GDN_DK):
        t = y[:, j * GDN_DK:(j + 1) * GDN_DK]
        parts.append(t * lax.rsqrt(jnp.sum(t * t, axis=-1, keepdims=True) + L2_EPS))
      y = jnp.concatenate(parts, axis=1)
      if col0 < GDN_KEY:
        q_ref[0, :, cs] = (y * (GDN_DK ** -0.5)).astype(BF16)
      else:
        k_ref[0, :, col0 - GDN_KEY:col0 - GDN_KEY + COL_TILE] = y.astype(BF16)
    else:
      v_ref[0, :, col0 - 2 * GDN_KEY:col0 - 2 * GDN_KEY + COL_TILE] = y.astype(BF16)
    if c % 2 == 1:
      gate_cols(c // 2)
  pbuf[0:halo, :] = pbuf[tm:tm + halo, :]

  ba = jnp.dot(xb, w_ref[:, GDN_QKV + GDN_VAL:], preferred_element_type=F32)
  beta_ref[0] = _sigmoid(ba[:, :H])
  g = -jnp.exp(alog_ref[...]) * _softplus(ba[:, H:] + dtb_ref[...])
  lmat = _chunk_tri(tm, lower=True)
  gc = jnp.zeros((tm, H), F32)
  for part in _split3(g):
    gc = gc + jnp.dot(lmat, part, preferred_element_type=F32)
  gcc_ref[0] = gc

  at = lax.dot_general(wat_ref[...], xb, (((1,), (1,)), ((), ())),
                       preferred_element_type=F32)
  gt = -jnp.exp(alogt_ref[...]) * _softplus(at + dtbt_ref[...])
  umat = _chunk_tri(tm, lower=False)
  gct = jnp.zeros((H, tm), F32)
  for part in _split3(gt):
    gct = gct + jnp.dot(part, umat, preferred_element_type=F32)
  even = gct[:GDN_PAIRS]
  odd = gct[GDN_PAIRS:]
  odd_r = pltpu.roll(gct, CHUNK, 1)[GDN_PAIRS:]
  even_r = pltpu.roll(gct, tm - CHUNK, 1)[:GDN_PAIRS]
  low_half = (lax.broadcasted_iota(jnp.int32, (GDN_PAIRS, tm), 1) & (V7X_LANES - 1)) < CHUNK
  p_even = jnp.where(low_half, even, odd_r)
  p_odd = jnp.where(low_half, even_r, odd)
  for j in range(tm // V7X_LANES):
    ls = slice(j * V7X_LANES, (j + 1) * V7X_LANES)
    gcr_ref[0, 2 * j] = p_even[:, ls]
    gcr_ref[0, 2 * j + 1] = p_odd[:, ls]


def _gdn_in_proj(xb, w, wat, convw, alog, dtb, alogt, dtbt):
  B, T, D = xb.shape
  tm = IN_ROWS
  H = GDN_HEADS
  const = lambda b, t: (0, 0)
  rowmap = lambda b, t: (b, t, 0)
  return pl.pallas_call(
      _gdn_in_kernel,
      grid=(B, T // tm),
      in_specs=[
          pl.BlockSpec((1, tm, D), rowmap),
          pl.BlockSpec(w.shape, const),
          pl.BlockSpec((H, D), const),
          pl.BlockSpec((CONV_K, GDN_QKV), const),
          pl.BlockSpec((1, H), const),
          pl.BlockSpec((1, H), const),
          pl.BlockSpec((H, 1), const),
          pl.BlockSpec((H, 1), const),
      ],
      out_specs=[
          pl.BlockSpec((1, tm, GDN_KEY), rowmap),
          pl.BlockSpec((1, tm, GDN_KEY), rowmap),
          pl.BlockSpec((1, tm, GDN_VAL), rowmap),
          pl.BlockSpec((1, tm, GDN_VAL), rowmap),
          pl.BlockSpec((1, tm, H), rowmap),
          pl.BlockSpec((1, tm, H), rowmap),
          pl.BlockSpec((1, tm // CHUNK, GDN_PAIRS, V7X_LANES), lambda b, t: (b, t, 0, 0)),
      ],
      out_shape=[
          jax.ShapeDtypeStruct((B, T, GDN_KEY), BF16),
          jax.ShapeDtypeStruct((B, T, GDN_KEY), BF16),
          jax.ShapeDtypeStruct((B, T, GDN_VAL), BF16),
          jax.ShapeDtypeStruct((B, T, GDN_VAL), BF16),
          jax.ShapeDtypeStruct((B, T, H), F32),
          jax.ShapeDtypeStruct((B, T, H), F32),
          jax.ShapeDtypeStruct((B, T // CHUNK, GDN_PAIRS, V7X_LANES), F32),
      ],
      scratch_shapes=[pltpu.VMEM((tm + V7X_SUBLANES, GDN_QKV), F32)],
      compiler_params=pltpu.CompilerParams(
          dimension_semantics=("arbitrary", "arbitrary"),
          vmem_limit_bytes=VMEM_LIMIT_BYTES),
      name="gdn_in_proj",
  )(xb, w, wat, convw, alog, dtb, alogt, dtbt)


def _block_diag(y, bdmask):
  n = y.shape[1] // CHUNK
  return jnp.where(bdmask, jnp.concatenate([y] * n, axis=0), jnp.zeros((), BF16))


def _packed_inverse_minus_eye(a_list, row, colp, bdmask, out):
  same16 = jnp.right_shift(row, 4) == jnp.right_shift(colp, 4)
  same32 = jnp.right_shift(row, 5) == jnp.right_shift(colp, 5)
  bd = lambda xs: [_block_diag(x.astype(BF16), bdmask) for x in xs]
  mm = lambda xs, ybs: [jnp.dot(x.astype(BF16), yb, preferred_element_type=F32)
                        for x, yb in zip(xs, ybs)]
  stack = lambda xs, ys: [jnp.concatenate([x, y], axis=0) for x, y in zip(xs, ys)]
  n = [jnp.where(same16, -a, 0.0) for a in a_list]
  s = mm(n, bd(n))
  yield
  p = n
  for _ in range(2):
    both = mm(stack(p, s), bd(s))
    yield
    p = [x + y + z[:CHUNK] for x, y, z in zip(p, s, both)]
    s = [z[CHUNK:] for z in both]
  ps = mm(p, bd(s))
  yield
  p = [x + y + z for x, y, z in zip(p, s, ps)]
  for mask in (same32 & jnp.logical_not(same16), jnp.logical_not(same32)):
    e = [jnp.where(mask, a, 0.0) for a in a_list]
    pe = mm(p, bd(e))
    yield
    f = [x + y for x, y in zip(e, pe)]
    fp = mm(f, bd(p))
    yield
    p = [x - y - z for x, y, z in zip(p, f, fp)]
  out.extend(p)


def _split_rows(x, low_half):
  return jnp.concatenate([jnp.where(low_half, x, 0.0), jnp.where(low_half, 0.0, x)], axis=0)


def _interleave(*gens):
  gens = list(gens)
  while gens:
    for g in list(gens):
      try:
        next(g)
      except StopIteration:
        gens.remove(g)


def _gdn_rec_kernel(q_ref, k_ref, v_ref, gate_ref, beta_ref, gcc_ref, gcr_ref,
                    nw_ref, o_ref, s_ref, uw_ref, qk_ref):
  @pl.when(pl.program_id(1) == 0)
  def _():
    s_ref[...] = jnp.zeros(s_ref.shape, F32)

  n_chunks = REC_ROWS // CHUNK
  heads = range(GDN_HEADS)
  nw = nw_ref[...]
  row = lax.broadcasted_iota(jnp.int32, (CHUNK, V7X_LANES), 0)
  lane = lax.broadcasted_iota(jnp.int32, (CHUNK, V7X_LANES), 1)
  colp = lane & (CHUNK - 1)
  low_half = lane < CHUNK
  incl = row >= colp
  strict = row > colp
  bd_r = lax.broadcasted_iota(jnp.int32, (2 * CHUNK, V7X_LANES), 0)
  bd_l = lax.broadcasted_iota(jnp.int32, (2 * CHUNK, V7X_LANES), 1)
  bdmask = (bd_r < CHUNK) == (bd_l < CHUNK)
  kbd_r = lax.broadcasted_iota(jnp.int32, (2 * CHUNK, 2 * GDN_DK), 0)
  kbd_l = lax.broadcasted_iota(jnp.int32, (2 * CHUNK, 2 * GDN_DK), 1)
  kbdmask = (kbd_r < CHUNK) == (kbd_l < GDN_DK)
  first_head_k = lax.broadcasted_iota(jnp.int32, (CHUNK, 2 * GDN_DK), 1) < GDN_DK

  rows_of = lambda c: slice(c * CHUNK, (c + 1) * CHUNK)
  klanes = lambda h: slice(h * GDN_DK, (h + 1) * GDN_DK)
  vlanes = lambda h: slice(h * GDN_DV, (h + 1) * GDN_DV)
  col_of = lambda ref, c, h: ref[0, rows_of(c), h:h + 1]

  def phase_a(chunks):
    probs = [(c, p) for c in chunks for p in range(GDN_PAIRS)]
    kb_list, a_list = [], []
    for c, p in probs:
      h0, h1 = 2 * p, 2 * p + 1
      pl2 = slice(h0 * GDN_DK, (h1 + 1) * GDN_DK)
      kp = k_ref[0, rows_of(c), pl2]
      beta_k = jnp.where(first_head_k, col_of(beta_ref, c, h0), col_of(beta_ref, c, h1))
      kb = kp.astype(F32) * beta_k
      lhs = jnp.concatenate([kb.astype(BF16), q_ref[0, rows_of(c), pl2]], axis=0)
      kbd = jnp.where(kbdmask, jnp.concatenate([kp, kp], axis=0), jnp.zeros((), BF16))
      aq = lax.dot_general(lhs, kbd, (((1,), (1,)), ((), ())), preferred_element_type=F32)
      gccp = jnp.where(low_half, col_of(gcc_ref, c, h0), col_of(gcc_ref, c, h1))
      gcrp = gcr_ref[0, c, p:p + 1, :]
      decay = jnp.where(incl, jnp.exp(jnp.where(incl, gccp - gcrp, 0.0)), 0.0)
      a_list.append(jnp.where(strict, aq[:CHUNK] * decay, 0.0))
      qk_ref[c, p] = aq[CHUNK:] * decay
      kb_list.append(kb)
    yield
    c_list = []
    yield from _packed_inverse_minus_eye(a_list, row, colp, bdmask, c_list)
    for (c, p), kb, cm in zip(probs, kb_list, c_list):
      rhs = []
      for j, h in enumerate((2 * p, 2 * p + 1)):
        beta_h = col_of(beta_ref, c, h)
        eg_h = jnp.exp(col_of(gcc_ref, c, h))
        rhs.append(jnp.concatenate(
            [v_ref[0, rows_of(c), vlanes(h)].astype(F32) * beta_h,
             kb[:, j * GDN_DK:(j + 1) * GDN_DK] * eg_h], axis=1))
      rhs = jnp.concatenate(rhs, axis=0)
      uw = rhs + _mm(_split_rows(cm, low_half), rhs)
      uw_ref[c, 2 * p] = uw[:CHUNK]
      uw_ref[c, 2 * p + 1] = uw[CHUNK:]
    yield

  def phase_b(chunks):
    for c in chunks:
      gcc = [col_of(gcc_ref, c, h) for h in heads]
      g_last = [x[CHUNK - 1:CHUNK, :] for x in gcc]
      r = []
      for h in heads:
        qd = q_ref[0, rows_of(c), klanes(h)].astype(F32) * jnp.exp(gcc[h])
        r.append(_mm(jnp.concatenate([uw_ref[c, h, :, GDN_DV:], qd], axis=0), s_ref[h]))
      yield
      v_new = [uw_ref[c, h, :, :GDN_DV] - r[h][:CHUNK] for h in heads]
      o = [None] * GDN_HEADS
      for p in range(GDN_PAIRS):
        h0, h1 = 2 * p, 2 * p + 1
        res = _mm(_split_rows(qk_ref[c, p], low_half),
                  jnp.concatenate([v_new[h0], v_new[h1]], axis=0))
        o[h0] = r[h0][CHUNK:] + res[:CHUNK]
        o[h1] = r[h1][CHUNK:] + res[CHUNK:]
      for h in heads:
        k_dec = k_ref[0, rows_of(c), klanes(h)].astype(F32) * jnp.exp(g_last[h] - gcc[h])
        s_ref[h] = s_ref[h] * jnp.exp(g_last[h]) + _mm_tn(k_dec, v_new[h])
      for h in heads:
        gate = gate_ref[0, rows_of(c), vlanes(h)].astype(F32)
        o_ref[0, rows_of(c), vlanes(h)] = _gated_rmsnorm(o[h], gate, nw).astype(BF16)
      yield

  first, second = range(n_chunks // 2), range(n_chunks // 2, n_chunks)
  _interleave(phase_a(first))
  _interleave(phase_a(second), phase_b(first))
  _interleave(phase_b(second))


def _gdn_recurrence(q, k, v, gate, beta, gcc, gcr, norm_w):
  B, T, _ = q.shape
  tt = REC_ROWS
  H = GDN_HEADS
  rowmap = lambda b, t: (b, t, 0)
  return pl.pallas_call(
      _gdn_rec_kernel,
      grid=(B, T // tt),
      in_specs=[
          pl.BlockSpec((1, tt, GDN_KEY), rowmap),
          pl.BlockSpec((1, tt, GDN_KEY), rowmap),
          pl.BlockSpec((1, tt, GDN_VAL), rowmap),
          pl.BlockSpec((1, tt, GDN_VAL), rowmap),
          pl.BlockSpec((1, tt, H), rowmap),
          pl.BlockSpec((1, tt, H), rowmap),
          pl.BlockSpec((1, tt // CHUNK, GDN_PAIRS, V7X_LANES), lambda b, t: (b, t, 0, 0)),
          pl.BlockSpec((1, GDN_DV), lambda b, t: (0, 0)),
      ],
      out_specs=pl.BlockSpec((1, tt, GDN_VAL), rowmap),
      out_shape=jax.ShapeDtypeStruct((B, T, GDN_VAL), BF16),
      scratch_shapes=[
          pltpu.VMEM((H, GDN_DK, GDN_DV), F32),
          pltpu.VMEM((tt // CHUNK, H, CHUNK, GDN_DV + GDN_DK), F32),
          pltpu.VMEM((tt // CHUNK, GDN_PAIRS, CHUNK, V7X_LANES), F32),
      ],
      compiler_params=pltpu.CompilerParams(
          dimension_semantics=("arbitrary", "arbitrary"),
          vmem_limit_bytes=VMEM_LIMIT_BYTES),
      name="gdn_recurrence",
  )(q, k, v, gate, beta, gcc, gcr, norm_w.reshape(1, GDN_DV))


GLA_ROWS = 256
GLA_Q0, GLA_K0, GLA_V0, GLA_G0 = 0, GLA_KEY, 2 * GLA_KEY, GLA_QKV


def _gla_chunk_all_heads(c, pv, bv, nw, o_ref, st_ref, b2_ref, kf_ref):
  heads = range(GLA_HEADS)
  nsub = CHUNK // SUB
  half = V7X_SUBLANES
  rows = slice(c * CHUNK, (c + 1) * CHUNK)
  ks = [slice(h * GLA_DK, (h + 1) * GLA_DK) for h in heads]
  vs = [slice(h * GLA_DV, (h + 1) * GLA_DV) for h in heads]
  lanes = lambda base, sl: slice(base + sl.start, base + sl.stop)
  q = [pv[rows, lanes(GLA_Q0, ks[h])].astype(F32) for h in heads]
  k = [pv[rows, lanes(GLA_K0, ks[h])].astype(F32) for h in heads]
  v = [pv[rows, lanes(GLA_V0, vs[h])] for h in heads]
  b2 = [bv[rows, ks[h]] * LOG2E for h in heads]
  b2_last = [x[CHUNK - 1:CHUNK, :] for x in b2]
  for h in heads:
    b2_ref[h] = b2[h]
    kf_ref[h] = k[h]
  o_inter = [_mm_nt(q[h] * jnp.exp2(b2[h]), st_ref[h]) for h in heads]

  rows8 = lax.broadcasted_iota(jnp.int32, (half, 1), 0)
  lane64 = lax.broadcasted_iota(jnp.int32, (half, CHUNK), 1)
  bands = [[None] * nsub for _ in heads]
  for h in heads:
    for i in range(nsub):
      r0 = i * SUB
      q_lo, q_hi = q[h][r0:r0 + half], q[h][r0 + half:r0 + SUB]
      b_lo, b_hi = b2[h][r0:r0 + half], b2[h][r0 + half:r0 + SUB]
      band_lo = jnp.zeros((half, CHUNK), F32)
      band_hi = jnp.zeros((half, CHUNK), F32)
      for s in range(SUB):
        b_s = b2_ref[h, r0 + s:r0 + s + 1, :]
        k_s = kf_ref[h, r0 + s:r0 + s + 1, :]
        if s < half:
          d = b_lo - b_s
          if s > 0:
            d = d + jnp.where(rows8 >= s, 0.0, NEG_BIG)
          col = jnp.sum(q_lo * k_s * jnp.exp2(d), axis=-1, keepdims=True)
          band_lo = jnp.where(lane64 == r0 + s, col, band_lo)
          d = b_hi - b_s
        else:
          d = b_hi - b_s
          if s > half:
            d = d + jnp.where(rows8 >= s - half, 0.0, NEG_BIG)
        col = jnp.sum(q_hi * k_s * jnp.exp2(d), axis=-1, keepdims=True)
        band_hi = jnp.where(lane64 == r0 + s, col, band_hi)
      bands[h][i] = jnp.concatenate([band_lo, band_hi], axis=0)
  for i in range(1, nsub):
    r0 = i * SUB
    for h in heads:
      ref = b2[h][r0:r0 + 1]
      qh = q[h][r0:r0 + SUB] * jnp.exp2(b2[h][r0:r0 + SUB] - ref)
      kh = jnp.concatenate([k[h][:r0] * jnp.exp2(ref - b2[h][:r0]),
                            jnp.zeros((CHUNK - r0, GLA_DK), F32)], axis=0)
      bands[h][i] = bands[h][i] + _mm_nt(qh, kh)
  o = [o_inter[h] + _mm(jnp.concatenate(bands[h], axis=0), v[h]) for h in heads]
  for h in heads:
    st_ref[h] = (st_ref[h] * jnp.exp2(b2_last[h])
                 + _mm_tn(v[h], k[h] * jnp.exp2(b2_last[h] - b2[h])))
  for h in heads:
    gate = pv[rows, lanes(GLA_G0, vs[h])].astype(F32)
    o_ref[0, rows, vs[h]] = _gated_rmsnorm(o[h], gate, nw).astype(BF16)


def _gla_mixer_kernel(x_ref, w_ref, wup_ref, bgk_ref, nw_ref, o_ref,
                      proj_ref, b_ref, st_ref, b2_ref, kf_ref):
  tm = GLA_ROWS
  n_chunks = tm // CHUNK
  piece = (GLA_QKV + GLA_VAL) // n_chunks

  @pl.when(pl.program_id(1) == 0)
  def _():
    st_ref[...] = jnp.zeros(st_ref.shape, F32)
    proj_ref[...] = jnp.zeros(proj_ref.shape, BF16)
    b_ref[...] = jnp.zeros(b_ref.shape, F32)

  cur = lax.rem(pl.program_id(1), 2)
  prev = 1 - cur
  pv_prev, bv_prev = proj_ref.at[prev], b_ref.at[prev]
  pv_cur, bv_cur = proj_ref.at[cur], b_ref.at[cur]
  nw = nw_ref[...]
  xb = x_ref[0]

  low = jnp.dot(xb, w_ref[:, GLA_QKV + GLA_VAL:], preferred_element_type=F32)
  lmat = _chunk_tri(CHUNK, lower=True)
  for c in range(n_chunks):
    rows = slice(c * CHUNK, (c + 1) * CHUNK)
    cols = slice(c * piece, (c + 1) * piece)
    acc = jnp.dot(xb, w_ref[:, cols], preferred_element_type=F32)
    logit = _mm(low[rows], wup_ref[...]) + bgk_ref[...]
    gk = (jnp.minimum(logit, 0.0) - jnp.log1p(jnp.exp(-jnp.abs(logit)))) * (1.0 / GLA_GATE_NORM)
    bsum = jnp.zeros((CHUNK, GLA_KEY), F32)
    for part in _split3(gk):
      bsum = bsum + jnp.dot(lmat, part, preferred_element_type=F32)
    _gla_chunk_all_heads(c, pv_prev, bv_prev, nw, o_ref, st_ref, b2_ref, kf_ref)
    n_q = max(0, min(piece, GLA_KEY - c * piece))
    if n_q:
      acc = jnp.concatenate([acc[:, :n_q] * (GLA_DK ** -0.5), acc[:, n_q:]], axis=1)
    pv_cur[:, cols] = acc.astype(BF16)
    bv_cur[rows, :] = bsum


def _gla_mixer_call(xb, w, wup, bgk, norm_w):
  B, T, D = xb.shape
  tm = GLA_ROWS
  n_t = T // tm
  const = lambda b, t: (0, 0)
  return pl.pallas_call(
      _gla_mixer_kernel,
      grid=(B, n_t + 1),
      in_specs=[
          pl.BlockSpec((1, tm, D), lambda b, t: (b, jnp.minimum(t, n_t - 1), 0)),
          pl.BlockSpec(w.shape, const),
          pl.BlockSpec((GLA_RANK, GLA_KEY), const),
          pl.BlockSpec((1, GLA_KEY), const),
          pl.BlockSpec((1, GLA_DV), const),
      ],
      out_specs=pl.BlockSpec((1, tm, GLA_VAL), lambda b, t: (b, jnp.maximum(t - 1, 0), 0)),
      out_shape=jax.ShapeDtypeStruct((B, T, GLA_VAL), BF16),
      scratch_shapes=[
          pltpu.VMEM((2, tm, GLA_QKV + GLA_VAL), BF16),
          pltpu.VMEM((2, tm, GLA_KEY), F32),
          pltpu.VMEM((GLA_HEADS, GLA_DV, GLA_DK), F32),
          pltpu.VMEM((GLA_HEADS, CHUNK, GLA_DK), F32),
          pltpu.VMEM((GLA_HEADS, CHUNK, GLA_DK), F32),
      ],
      compiler_params=pltpu.CompilerParams(
          dimension_semantics=("arbitrary", "arbitrary"),
          vmem_limit_bytes=VMEM_LIMIT_BYTES),
      name="gla_mixer",
  )(xb, w, wup, bgk, norm_w.reshape(1, GLA_DV))


def _out_kernel(o_ref, x_ref, w_ref, g_ref, b_ref, y_ref, *maybe_yb_ref):
  nblk = OUT_ROWS // OUT_BLOCK
  rows_of = lambda i: slice(i * OUT_BLOCK, (i + 1) * OUT_BLOCK)
  proj = lambda i: jnp.dot(o_ref[rows_of(i), :], w_ref[...], preferred_element_type=F32)
  y = proj(0)
  for i in range(nblk):
    y_next = proj(i + 1) if i + 1 < nblk else None
    z = DEEP_ALPHA * x_ref[rows_of(i), :] + y
    mu = jnp.mean(z, axis=-1, keepdims=True)
    zc = z - mu
    var = jnp.mean(zc * zc, axis=-1, keepdims=True)
    out = zc * lax.rsqrt(var + LN_EPS) * g_ref[...] + b_ref[...]
    y_ref[rows_of(i), :] = out
    for yb_ref in maybe_yb_ref:
      yb_ref[rows_of(i), :] = out.astype(BF16)
    y = y_next


def _out_proj_norm(o, x, w_out, ln_g, ln_b, with_bf16_copy):
  n, dv = o.shape
  tm = OUT_ROWS
  const = lambda i: (0, 0)
  rowmap = lambda i: (i, 0)
  n_out = 2 if with_bf16_copy else 1
  outs = pl.pallas_call(
      _out_kernel,
      grid=(n // tm,),
      in_specs=[
          pl.BlockSpec((tm, dv), rowmap),
          pl.BlockSpec((tm, D_MODEL), rowmap),
          pl.BlockSpec((dv, D_MODEL), const),
          pl.BlockSpec((1, D_MODEL), const),
          pl.BlockSpec((1, D_MODEL), const),
      ],
      out_specs=[pl.BlockSpec((tm, D_MODEL), rowmap)] * n_out,
      out_shape=[jax.ShapeDtypeStruct((n, D_MODEL), F32),
                 jax.ShapeDtypeStruct((n, D_MODEL), BF16)][:n_out],
      compiler_params=pltpu.CompilerParams(
          dimension_semantics=("arbitrary",),
          vmem_limit_bytes=VMEM_LIMIT_BYTES),
      name="out_proj_norm",
  )(o, x, w_out, ln_g.reshape(1, D_MODEL), ln_b.reshape(1, D_MODEL))
  return outs[0], (outs[1] if with_bf16_copy else None)


def _gdn_mixer(xb, w_in, conv_w, a_log, dt_bias, norm_w):
  H = GDN_HEADS
  order = jnp.array(list(range(0, H, 2)) + list(range(1, H, 2)), jnp.int32)
  wat = w_in[:, GDN_QKV + GDN_VAL + H:].T[order].astype(BF16)
  q, k, v, gate, beta, gcc, gcr = _gdn_in_proj(
      xb, w_in.astype(BF16), wat, conv_w, a_log.reshape(1, H), dt_bias.reshape(1, H),
      a_log[order].reshape(H, 1), dt_bias[order].reshape(H, 1))
  return _gdn_recurrence(q, k, v, gate, beta, gcc, gcr, norm_w)


def _gla_mixer(xb, w_in, w_gk_up, b_gk, norm_w):
  return _gla_mixer_call(xb, w_in.astype(BF16), w_gk_up.astype(BF16),
                         b_gk.reshape(1, GLA_KEY), norm_w)


def kernel(x, gdn_w_in, gdn_conv_w, gdn_a_log, gdn_dt_bias, gdn_norm_w, gdn_w_out,
           gla_w_in, gla_w_gk_up, gla_b_gk, gla_norm_w, gla_w_out, ln_g, ln_b):
  B, T, D = x.shape
  xb = x.astype(BF16)
  x = x.reshape(B * T, D)
  for i in range(DEPTH):
    j = i // 2
    if i % 2 == 0:
      o = _gdn_mixer(xb, gdn_w_in[j], gdn_conv_w[j], gdn_a_log[j], gdn_dt_bias[j],
                     gdn_norm_w[j])
      w_out = gdn_w_out[j]
    else:
      o = _gla_mixer(xb, gla_w_in[j], gla_w_gk_up[j], gla_b_gk[j], gla_norm_w[j])
      w_out = gla_w_out[j]
    x, xb = _out_proj_norm(o.reshape(B * T, -1), x, w_out.astype(BF16), ln_g[i], ln_b[i],
                           with_bf16_copy=i + 1 < DEPTH)
    if xb is not None:
      xb = xb.reshape(B, T, D)
  return x.reshape(B, T, D)
```

```python
import jax
import jax.numpy as jnp
from jax import lax
from jax.experimental import pallas as pl
from jax.experimental.pallas import tpu as pltpu

F32 = jnp.float32
BF16 = jnp.bfloat16

D_MODEL = 1024
DEPTH = 4
CHUNK = 64
CHUNK_SHIFT = 6
CONV_K = 4

GDN_HEADS = 8
GDN_DK = 128
GDN_DV = 256
GDN_KEY = GDN_HEADS * GDN_DK
GDN_VAL = GDN_HEADS * GDN_DV
GDN_QKV = 2 * GDN_KEY + GDN_VAL
GDN_PAIRS = GDN_HEADS // 2

GLA_HEADS = 4
GLA_DK = 128
GLA_DV = 256
GLA_KEY = GLA_HEADS * GLA_DK
GLA_VAL = GLA_HEADS * GLA_DV
GLA_QKV = 2 * GLA_KEY + GLA_VAL
GLA_RANK = 16
GLA_GATE_NORM = 16.0

DEEP_ALPHA = (2.0 * DEPTH) ** 0.25
LN_EPS = 1e-5
RMS_EPS = 1e-6
L2_EPS = 1e-6
LOG2E = 1.4426950408889634

V7X_MXU_WIDTH = 256
V7X_SUBLANES = 8
V7X_LANES = 128
VMEM_LIMIT_BYTES = 56 * 1024 * 1024

IN_ROWS = 256
REC_ROWS = 512
OUT_ROWS = 1024
OUT_BLOCK = 256
COL_TILE = V7X_MXU_WIDTH
SUB = 16
NEG_BIG = -1e30


def _mm(a, b):
  return jnp.dot(a.astype(BF16), b.astype(BF16), preferred_element_type=F32)


def _mm_nt(a, b):
  return lax.dot_general(a.astype(BF16), b.astype(BF16),
                         (((1,), (1,)), ((), ())), preferred_element_type=F32)


def _mm_tn(a, b):
  return lax.dot_general(a.astype(BF16), b.astype(BF16),
                         (((0,), (0,)), ((), ())), preferred_element_type=F32)


def _split3(x):
  hi = x.astype(BF16)
  r1 = x - hi.astype(F32)
  mid = r1.astype(BF16)
  lo = (r1 - mid.astype(F32)).astype(BF16)
  return hi, mid, lo


def _sigmoid(x):
  return 1.0 / (1.0 + jnp.exp(-x))


def _softplus(x):
  return jnp.maximum(x, 0.0) + jnp.log1p(jnp.exp(-jnp.abs(x)))


def _silu(x):
  return x * _sigmoid(x)


def _chunk_tri(n, lower):
  r = lax.broadcasted_iota(jnp.int32, (n, n), 0)
  c = lax.broadcasted_iota(jnp.int32, (n, n), 1)
  same = jnp.right_shift(r, CHUNK_SHIFT) == jnp.right_shift(c, CHUNK_SHIFT)
  tri = (c <= r) if lower else (r <= c)
  return jnp.where(same & tri, 1.0, 0.0).astype(BF16)


def _gated_rmsnorm(o, gate, nw):
  ms = jnp.mean(o * o, axis=-1, keepdims=True)
  return o * lax.rsqrt(ms + RMS_EPS) * nw * _silu(gate)


def _gdn_in_kernel(x_ref, w_ref, wat_ref, convw_ref, alog_ref, dtb_ref, alogt_ref, dtbt_ref,
                   q_ref, k_ref, v_ref, gate_ref, beta_ref, gcc_ref, gcr_ref, pbuf):
  tm = IN_ROWS
  halo = V7X_SUBLANES
  H = GDN_HEADS

  @pl.when(pl.program_id(1) == 0)
  def _():
    pbuf[0:halo, :] = jnp.zeros((halo, GDN_QKV), F32)

  xb = x_ref[0].astype(BF16)

  def gate_cols(c):
    cs = slice(c * COL_TILE, (c + 1) * COL_TILE)
    ws = slice(GDN_QKV + c * COL_TILE, GDN_QKV + (c + 1) * COL_TILE)
    gate_ref[0, :, cs] = jnp.dot(xb, w_ref[0, :, ws], preferred_element_type=F32).astype(BF16)

  n_cols = GDN_QKV // COL_TILE
  proj = lambda c: jnp.dot(xb, w_ref[0, :, c * COL_TILE:(c + 1) * COL_TILE],
                           preferred_element_type=F32)
  acc_next = proj(0)
  for c in range(n_cols):
    cs = slice(c * COL_TILE, (c + 1) * COL_TILE)
    acc = acc_next
    pbuf[halo:halo + tm, cs] = acc
    if c + 1 < n_cols:
      acc_next = proj(c + 1)
    cw = convw_ref[:, cs]
    y = acc * cw[CONV_K - 1:CONV_K]
    for j in range(CONV_K - 1):
      off = halo - (CONV_K - 1) + j
      y = y + pbuf[off:off + tm, cs] * cw[j:j + 1]
    y = _silu(y)
    col0 = c * COL_TILE
    if col0 < 2 * GDN_KEY:
      parts = []
      for j in range(COL_TILE // GDN_DK):
        t = y[:, j * GDN_DK:(j + 1) * GDN_DK]
        parts.append(t * lax.rsqrt(jnp.sum(t * t, axis=-1, keepdims=True) + L2_EPS))
      y = jnp.concatenate(parts, axis=1)
      if col0 < GDN_KEY:
        q_ref[0, :, cs] = (y * (GDN_DK ** -0.5)).astype(BF16)
      else:
        k_ref[0, :, col0 - GDN_KEY:col0 - GDN_KEY + COL_TILE] = y.astype(BF16)
    else:
      v_ref[0, :, col0 - 2 * GDN_KEY:col0 - 2 * GDN_KEY + COL_TILE] = y.astype(BF16)
    if c % 2 == 1:
      gate_cols(c // 2)
  pbuf[0:halo, :] = pbuf[tm:tm + halo, :]

  ba = jnp.dot(xb, w_ref[0, :, GDN_QKV + GDN_VAL:], preferred_element_type=F32)
  beta_ref[0] = _sigmoid(ba[:, :H])
  g = -jnp.exp(alog_ref[...]) * _softplus(ba[:, H:] + dtb_ref[...])
  lmat = _chunk_tri(tm, lower=True)
  gc = jnp.zeros((tm, H), F32)
  for part in _split3(g):
    gc = gc + jnp.dot(lmat, part, preferred_element_type=F32)
  gcc_ref[0] = gc

  at = lax.dot_general(wat_ref[...], xb, (((1,), (1,)), ((), ())),
                       preferred_element_type=F32)
  gt = -jnp.exp(alogt_ref[...]) * _softplus(at + dtbt_ref[...])
  umat = _chunk_tri(tm, lower=False)
  gct = jnp.zeros((H, tm), F32)
  for part in _split3(gt):
    gct = gct + jnp.dot(part, umat, preferred_element_type=F32)
  even = gct[:GDN_PAIRS]
  odd = gct[GDN_PAIRS:]
  odd_r = pltpu.roll(gct, CHUNK, 1)[GDN_PAIRS:]
  even_r = pltpu.roll(gct, tm - CHUNK, 1)[:GDN_PAIRS]
  low_half = (lax.broadcasted_iota(jnp.int32, (GDN_PAIRS, tm), 1) & (V7X_LANES - 1)) < CHUNK
  p_even = jnp.where(low_half, even, odd_r)
  p_odd = jnp.where(low_half, even_r, odd)
  for j in range(tm // V7X_LANES):
    ls = slice(j * V7X_LANES, (j + 1) * V7X_LANES)
    gcr_ref[0, 2 * j] = p_even[:, ls]
    gcr_ref[0, 2 * j + 1] = p_odd[:, ls]


def _gdn_in_proj(xb, w, layer, wat, convw, alog, dtb, alogt, dtbt):
  B, T, D = xb.shape
  tm = IN_ROWS
  H = GDN_HEADS
  const = lambda b, t: (0, 0)
  rowmap = lambda b, t: (b, t, 0)
  return pl.pallas_call(
      _gdn_in_kernel,
      grid=(B, T // tm),
      in_specs=[
          pl.BlockSpec((1, tm, D), rowmap),
          pl.BlockSpec((1,) + w.shape[1:], lambda b, t: (layer, 0, 0)),
          pl.BlockSpec((H, D), const),
          pl.BlockSpec((CONV_K, GDN_QKV), const),
          pl.BlockSpec((1, H), const),
          pl.BlockSpec((1, H), const),
          pl.BlockSpec((H, 1), const),
          pl.BlockSpec((H, 1), const),
      ],
      out_specs=[
          pl.BlockSpec((1, tm, GDN_KEY), rowmap),
          pl.BlockSpec((1, tm, GDN_KEY), rowmap),
          pl.BlockSpec((1, tm, GDN_VAL), rowmap),
          pl.BlockSpec((1, tm, GDN_VAL), rowmap),
          pl.BlockSpec((1, tm, H), rowmap),
          pl.BlockSpec((1, tm, H), rowmap),
          pl.BlockSpec((1, tm // CHUNK, GDN_PAIRS, V7X_LANES), lambda b, t: (b, t, 0, 0)),
      ],
      out_shape=[
          jax.ShapeDtypeStruct((B, T, GDN_KEY), BF16),
          jax.ShapeDtypeStruct((B, T, GDN_KEY), BF16),
          jax.ShapeDtypeStruct((B, T, GDN_VAL), BF16),
          jax.ShapeDtypeStruct((B, T, GDN_VAL), BF16),
          jax.ShapeDtypeStruct((B, T, H), F32),
          jax.ShapeDtypeStruct((B, T, H), F32),
          jax.ShapeDtypeStruct((B, T // CHUNK, GDN_PAIRS, V7X_LANES), F32),
      ],
      scratch_shapes=[pltpu.VMEM((tm + V7X_SUBLANES, GDN_QKV), F32)],
      compiler_params=pltpu.CompilerParams(
          dimension_semantics=("arbitrary", "arbitrary"),
          vmem_limit_bytes=VMEM_LIMIT_BYTES),
      name="gdn_in_proj",
  )(xb, w, wat, convw, alog, dtb, alogt, dtbt)


def _block_diag(y, bdmask):
  n = y.shape[1] // CHUNK
  return jnp.where(bdmask, jnp.concatenate([y] * n, axis=0), jnp.zeros((), BF16))


def _packed_inverse_minus_eye(a_list, row, colp, bdmask, out):
  same16 = jnp.right_shift(row, 4) == jnp.right_shift(colp, 4)
  same32 = jnp.right_shift(row, 5) == jnp.right_shift(colp, 5)
  bd = lambda xs: [_block_diag(x.astype(BF16), bdmask) for x in xs]
  mm = lambda xs, ybs: [jnp.dot(x.astype(BF16), yb, preferred_element_type=F32)
                        for x, yb in zip(xs, ybs)]
  stack = lambda xs, ys: [jnp.concatenate([x, y], axis=0) for x, y in zip(xs, ys)]
  n = [jnp.where(same16, -a, 0.0) for a in a_list]
  s = mm(n, bd(n))
  yield
  p = n
  for _ in range(2):
    both = mm(stack(p, s), bd(s))
    yield
    p = [x + y + z[:CHUNK] for x, y, z in zip(p, s, both)]
    s = [z[CHUNK:] for z in both]
  ps = mm(p, bd(s))
  yield
  p = [x + y + z for x, y, z in zip(p, s, ps)]
  for mask in (same32 & jnp.logical_not(same16), jnp.logical_not(same32)):
    e = [jnp.where(mask, a, 0.0) for a in a_list]
    pe = mm(p, bd(e))
    yield
    f = [x + y for x, y in zip(e, pe)]
    fp = mm(f, bd(p))
    yield
    p = [x - y - z for x, y, z in zip(p, f, fp)]
  out.extend(p)


def _split_rows(x, low_half):
  return jnp.concatenate([jnp.where(low_half, x, 0.0), jnp.where(low_half, 0.0, x)], axis=0)


def _interleave(*gens):
  gens = list(gens)
  while gens:
    for g in list(gens):
      try:
        next(g)
      except StopIteration:
        gens.remove(g)


def _gdn_rec_kernel(q_ref, k_ref, v_ref, gate_ref, beta_ref, gcc_ref, gcr_ref,
                    nw_ref, o_ref, s_ref, uw_ref, qk_ref):
  @pl.when(pl.program_id(1) == 0)
  def _():
    s_ref[...] = jnp.zeros(s_ref.shape, F32)

  n_chunks = REC_ROWS // CHUNK
  heads = range(GDN_HEADS)
  nw = nw_ref[...]
  row = lax.broadcasted_iota(jnp.int32, (CHUNK, V7X_LANES), 0)
  lane = lax.broadcasted_iota(jnp.int32, (CHUNK, V7X_LANES), 1)
  colp = lane & (CHUNK - 1)
  low_half = lane < CHUNK
  incl = row >= colp
  strict = row > colp
  bd_r = lax.broadcasted_iota(jnp.int32, (2 * CHUNK, V7X_LANES), 0)
  bd_l = lax.broadcasted_iota(jnp.int32, (2 * CHUNK, V7X_LANES), 1)
  bdmask = (bd_r < CHUNK) == (bd_l < CHUNK)
  kbd_r = lax.broadcasted_iota(jnp.int32, (2 * CHUNK, 2 * GDN_DK), 0)
  kbd_l = lax.broadcasted_iota(jnp.int32, (2 * CHUNK, 2 * GDN_DK), 1)
  kbdmask = (kbd_r < CHUNK) == (kbd_l < GDN_DK)
  first_head_k = lax.broadcasted_iota(jnp.int32, (CHUNK, 2 * GDN_DK), 1) < GDN_DK

  rows_of = lambda c: slice(c * CHUNK, (c + 1) * CHUNK)
  klanes = lambda h: slice(h * GDN_DK, (h + 1) * GDN_DK)
  vlanes = lambda h: slice(h * GDN_DV, (h + 1) * GDN_DV)
  col_of = lambda ref, c, h: ref[0, rows_of(c), h:h + 1]

  def phase_a(chunks):
    probs = [(c, p) for c in chunks for p in range(GDN_PAIRS)]
    kb_list, a_list = [], []
    for c, p in probs:
      h0, h1 = 2 * p, 2 * p + 1
      pl2 = slice(h0 * GDN_DK, (h1 + 1) * GDN_DK)
      kp = k_ref[0, rows_of(c), pl2]
      beta_k = jnp.where(first_head_k, col_of(beta_ref, c, h0), col_of(beta_ref, c, h1))
      kb = kp.astype(F32) * beta_k
      lhs = jnp.concatenate([kb.astype(BF16), q_ref[0, rows_of(c), pl2]], axis=0)
      kbd = jnp.where(kbdmask, jnp.concatenate([kp, kp], axis=0), jnp.zeros((), BF16))
      aq = lax.dot_general(lhs, kbd, (((1,), (1,)), ((), ())), preferred_element_type=F32)
      gccp = jnp.where(low_half, col_of(gcc_ref, c, h0), col_of(gcc_ref, c, h1))
      gcrp = gcr_ref[0, c, p:p + 1, :]
      decay = jnp.where(incl, jnp.exp(jnp.where(incl, gccp - gcrp, 0.0)), 0.0)
      a_list.append(jnp.where(strict, aq[:CHUNK] * decay, 0.0))
      qk_ref[c, p] = aq[CHUNK:] * decay
      kb_list.append(kb)
    yield
    c_list = []
    yield from _packed_inverse_minus_eye(a_list, row, colp, bdmask, c_list)
    for (c, p), kb, cm in zip(probs, kb_list, c_list):
      rhs = []
      for j, h in enumerate((2 * p, 2 * p + 1)):
        beta_h = col_of(beta_ref, c, h)
        eg_h = jnp.exp(col_of(gcc_ref, c, h))
        rhs.append(jnp.concatenate(
            [v_ref[0, rows_of(c), vlanes(h)].astype(F32) * beta_h,
             kb[:, j * GDN_DK:(j + 1) * GDN_DK] * eg_h], axis=1))
      rhs = jnp.concatenate(rhs, axis=0)
      uw = rhs + _mm(_split_rows(cm, low_half), rhs)
      uw_ref[c, 2 * p] = uw[:CHUNK]
      uw_ref[c, 2 * p + 1] = uw[CHUNK:]
    yield

  def phase_b(chunks):
    for c in chunks:
      gcc = [col_of(gcc_ref, c, h) for h in heads]
      g_last = [x[CHUNK - 1:CHUNK, :] for x in gcc]
      r = []
      for h in heads:
        qd = q_ref[0, rows_of(c), klanes(h)].astype(F32) * jnp.exp(gcc[h])
        r.append(_mm(jnp.concatenate([uw_ref[c, h, :, GDN_DV:], qd], axis=0), s_ref[h]))
      yield
      v_new = [uw_ref[c, h, :, :GDN_DV] - r[h][:CHUNK] for h in heads]
      o = [None] * GDN_HEADS
      for p in range(GDN_PAIRS):
        h0, h1 = 2 * p, 2 * p + 1
        res = _mm(_split_rows(qk_ref[c, p], low_half),
                  jnp.concatenate([v_new[h0], v_new[h1]], axis=0))
        o[h0] = r[h0][CHUNK:] + res[:CHUNK]
        o[h1] = r[h1][CHUNK:] + res[CHUNK:]
      for h in heads:
        k_dec = k_ref[0, rows_of(c), klanes(h)].astype(F32) * jnp.exp(g_last[h] - gcc[h])
        s_ref[h] = s_ref[h] * jnp.exp(g_last[h]) + _mm_tn(k_dec, v_new[h])
      for h in heads:
        gate = gate_ref[0, rows_of(c), vlanes(h)].astype(F32)
        o_ref[0, rows_of(c), vlanes(h)] = _gated_rmsnorm(o[h], gate, nw).astype(BF16)
      yield

  first, second = range(n_chunks // 2), range(n_chunks // 2, n_chunks)
  _interleave(phase_a(first))
  _interleave(phase_a(second), phase_b(first))
  _interleave(phase_b(second))


def _gdn_recurrence(q, k, v, gate, beta, gcc, gcr, norm_w):
  B, T, _ = q.shape
  tt = REC_ROWS
  H = GDN_HEADS
  rowmap = lambda b, t: (b, t, 0)
  return pl.pallas_call(
      _gdn_rec_kernel,
      grid=(B, T // tt),
      in_specs=[
          pl.BlockSpec((1, tt, GDN_KEY), rowmap),
          pl.BlockSpec((1, tt, GDN_KEY), rowmap),
          pl.BlockSpec((1, tt, GDN_VAL), rowmap),
          pl.BlockSpec((1, tt, GDN_VAL), rowmap),
          pl.BlockSpec((1, tt, H), rowmap),
          pl.BlockSpec((1, tt, H), rowmap),
          pl.BlockSpec((1, tt // CHUNK, GDN_PAIRS, V7X_LANES), lambda b, t: (b, t, 0, 0)),
          pl.BlockSpec((1, GDN_DV), lambda b, t: (0, 0)),
      ],
      out_specs=pl.BlockSpec((1, tt, GDN_VAL), rowmap),
      out_shape=jax.ShapeDtypeStruct((B, T, GDN_VAL), BF16),
      scratch_shapes=[
          pltpu.VMEM((H, GDN_DK, GDN_DV), F32),
          pltpu.VMEM((tt // CHUNK, H, CHUNK, GDN_DV + GDN_DK), F32),
          pltpu.VMEM((tt // CHUNK, GDN_PAIRS, CHUNK, V7X_LANES), F32),
      ],
      compiler_params=pltpu.CompilerParams(
          dimension_semantics=("arbitrary", "arbitrary"),
          vmem_limit_bytes=VMEM_LIMIT_BYTES),
      name="gdn_recurrence",
  )(q, k, v, gate, beta, gcc, gcr, norm_w.reshape(1, GDN_DV))


GLA_ROWS = 256
GLA_Q0, GLA_K0, GLA_V0, GLA_G0 = 0, GLA_KEY, 2 * GLA_KEY, GLA_QKV


def _gla_chunk_all_heads(c, pv, bv, nw, o_ref, st_ref, b2_ref, kf_ref):
  heads = range(GLA_HEADS)
  nsub = CHUNK // SUB
  half = V7X_SUBLANES
  rows = slice(c * CHUNK, (c + 1) * CHUNK)
  ks = [slice(h * GLA_DK, (h + 1) * GLA_DK) for h in heads]
  vs = [slice(h * GLA_DV, (h + 1) * GLA_DV) for h in heads]
  lanes = lambda base, sl: slice(base + sl.start, base + sl.stop)
  q = [pv[rows, lanes(GLA_Q0, ks[h])].astype(F32) for h in heads]
  k = [pv[rows, lanes(GLA_K0, ks[h])].astype(F32) for h in heads]
  v = [pv[rows, lanes(GLA_V0, vs[h])] for h in heads]
  b2 = [bv[rows, ks[h]] * LOG2E for h in heads]
  b2_last = [x[CHUNK - 1:CHUNK, :] for x in b2]
  for h in heads:
    b2_ref[h] = b2[h]
    kf_ref[h] = k[h]
  o_inter = [_mm_nt(q[h] * jnp.exp2(b2[h]), st_ref[h]) for h in heads]

  rows8 = lax.broadcasted_iota(jnp.int32, (half, 1), 0)
  lane64 = lax.broadcasted_iota(jnp.int32, (half, CHUNK), 1)
  bands = [[None] * nsub for _ in heads]
  for h in heads:
    for i in range(nsub):
      r0 = i * SUB
      q_lo, q_hi = q[h][r0:r0 + half], q[h][r0 + half:r0 + SUB]
      b_lo, b_hi = b2[h][r0:r0 + half], b2[h][r0 + half:r0 + SUB]
      band_lo = jnp.zeros((half, CHUNK), F32)
      band_hi = jnp.zeros((half, CHUNK), F32)
      for s in range(SUB):
        b_s = b2_ref[h, r0 + s:r0 + s + 1, :]
        k_s = kf_ref[h, r0 + s:r0 + s + 1, :]
        if s < half:
          d = b_lo - b_s
          if s > 0:
            d = d + jnp.where(rows8 >= s, 0.0, NEG_BIG)
          col = jnp.sum(q_lo * k_s * jnp.exp2(d), axis=-1, keepdims=True)
          band_lo = jnp.where(lane64 == r0 + s, col, band_lo)
          d = b_hi - b_s
        else:
          d = b_hi - b_s
          if s > half:
            d = d + jnp.where(rows8 >= s - half, 0.0, NEG_BIG)
        col = jnp.sum(q_hi * k_s * jnp.exp2(d), axis=-1, keepdims=True)
        band_hi = jnp.where(lane64 == r0 + s, col, band_hi)
      bands[h][i] = jnp.concatenate([band_lo, band_hi], axis=0)
  for i in range(1, nsub):
    r0 = i * SUB
    for h in heads:
      ref = b2[h][r0:r0 + 1]
      qh = q[h][r0:r0 + SUB] * jnp.exp2(b2[h][r0:r0 + SUB] - ref)
      kh = jnp.concatenate([k[h][:r0] * jnp.exp2(ref - b2[h][:r0]),
                            jnp.zeros((CHUNK - r0, GLA_DK), F32)], axis=0)
      bands[h][i] = bands[h][i] + _mm_nt(qh, kh)
  o = [o_inter[h] + _mm(jnp.concatenate(bands[h], axis=0), v[h]) for h in heads]
  for h in heads:
    st_ref[h] = (st_ref[h] * jnp.exp2(b2_last[h])
                 + _mm_tn(v[h], k[h] * jnp.exp2(b2_last[h] - b2[h])))
  for h in heads:
    gate = pv[rows, lanes(GLA_G0, vs[h])].astype(F32)
    o_ref[0, rows, vs[h]] = _gated_rmsnorm(o[h], gate, nw).astype(BF16)


def _gla_mixer_kernel(x_ref, w_ref, wup_ref, bgk_ref, nw_ref, o_ref,
                      proj_ref, b_ref, st_ref, b2_ref, kf_ref):
  tm = GLA_ROWS
  n_chunks = tm // CHUNK
  piece = (GLA_QKV + GLA_VAL) // n_chunks

  @pl.when(pl.program_id(1) == 0)
  def _():
    st_ref[...] = jnp.zeros(st_ref.shape, F32)
    proj_ref[...] = jnp.zeros(proj_ref.shape, BF16)
    b_ref[...] = jnp.zeros(b_ref.shape, F32)

  cur = lax.rem(pl.program_id(1), 2)
  prev = 1 - cur
  pv_prev, bv_prev = proj_ref.at[prev], b_ref.at[prev]
  pv_cur, bv_cur = proj_ref.at[cur], b_ref.at[cur]
  nw = nw_ref[...]
  xb = x_ref[0]

  low = jnp.dot(xb, w_ref[0, :, GLA_QKV + GLA_VAL:], preferred_element_type=F32)
  lmat = _chunk_tri(CHUNK, lower=True)
  for c in range(n_chunks):
    rows = slice(c * CHUNK, (c + 1) * CHUNK)
    cols = slice(c * piece, (c + 1) * piece)
    acc = jnp.dot(xb, w_ref[0, :, cols], preferred_element_type=F32)
    logit = _mm(low[rows], wup_ref[...]) + bgk_ref[...]
    gk = (jnp.minimum(logit, 0.0) - jnp.log1p(jnp.exp(-jnp.abs(logit)))) * (1.0 / GLA_GATE_NORM)
    bsum = jnp.zeros((CHUNK, GLA_KEY), F32)
    for part in _split3(gk):
      bsum = bsum + jnp.dot(lmat, part, preferred_element_type=F32)
    _gla_chunk_all_heads(c, pv_prev, bv_prev, nw, o_ref, st_ref, b2_ref, kf_ref)
    n_q = max(0, min(piece, GLA_KEY - c * piece))
    if n_q == piece:
      acc = acc * (GLA_DK ** -0.5)
    elif n_q:
      acc = jnp.concatenate([acc[:, :n_q] * (GLA_DK ** -0.5), acc[:, n_q:]], axis=1)
    pv_cur[:, cols] = acc.astype(BF16)
    bv_cur[rows, :] = bsum


def _gla_mixer_call(xb, w, layer, wup, bgk, norm_w):
  B, T, D = xb.shape
  tm = GLA_ROWS
  n_t = T // tm
  const = lambda b, t: (0, 0)
  return pl.pallas_call(
      _gla_mixer_kernel,
      grid=(B, n_t + 1),
      in_specs=[
          pl.BlockSpec((1, tm, D), lambda b, t: (b, jnp.minimum(t, n_t - 1), 0)),
          pl.BlockSpec((1,) + w.shape[1:], lambda b, t: (layer, 0, 0)),
          pl.BlockSpec((GLA_RANK, GLA_KEY), const),
          pl.BlockSpec((1, GLA_KEY), const),
          pl.BlockSpec((1, GLA_DV), const),
      ],
      out_specs=pl.BlockSpec((1, tm, GLA_VAL), lambda b, t: (b, jnp.maximum(t - 1, 0), 0)),
      out_shape=jax.ShapeDtypeStruct((B, T, GLA_VAL), BF16),
      scratch_shapes=[
          pltpu.VMEM((2, tm, GLA_QKV + GLA_VAL), BF16),
          pltpu.VMEM((2, tm, GLA_KEY), F32),
          pltpu.VMEM((GLA_HEADS, GLA_DV, GLA_DK), F32),
          pltpu.VMEM((GLA_HEADS, CHUNK, GLA_DK), F32),
          pltpu.VMEM((GLA_HEADS, CHUNK, GLA_DK), F32),
      ],
      compiler_params=pltpu.CompilerParams(
          dimension_semantics=("arbitrary", "arbitrary"),
          vmem_limit_bytes=VMEM_LIMIT_BYTES),
      name="gla_mixer",
  )(xb, w, wup, bgk, norm_w.reshape(1, GLA_DV))


def _out_kernel(o_ref, x_ref, w_ref, g_ref, b_ref, y_ref, *maybe_yb_ref):
  nblk = OUT_ROWS // OUT_BLOCK
  rows_of = lambda i: slice(i * OUT_BLOCK, (i + 1) * OUT_BLOCK)
  proj = lambda i: jnp.dot(o_ref[rows_of(i), :], w_ref[0], preferred_element_type=F32)
  y = proj(0)
  for i in range(nblk):
    y_next = proj(i + 1) if i + 1 < nblk else None
    z = DEEP_ALPHA * x_ref[rows_of(i), :] + y
    mu = jnp.mean(z, axis=-1, keepdims=True)
    zc = z - mu
    var = jnp.mean(zc * zc, axis=-1, keepdims=True)
    out = zc * lax.rsqrt(var + LN_EPS) * g_ref[...] + b_ref[...]
    y_ref[rows_of(i), :] = out
    for yb_ref in maybe_yb_ref:
      yb_ref[rows_of(i), :] = out.astype(BF16)
    y = y_next


def _out_proj_norm(o, x, w_out, layer, ln_g, ln_b, with_bf16_copy):
  n, dv = o.shape
  tm = OUT_ROWS
  const = lambda i: (0, 0)
  rowmap = lambda i: (i, 0)
  n_out = 2 if with_bf16_copy else 1
  outs = pl.pallas_call(
      _out_kernel,
      grid=(n // tm,),
      in_specs=[
          pl.BlockSpec((tm, dv), rowmap),
          pl.BlockSpec((tm, D_MODEL), rowmap),
          pl.BlockSpec((1, dv, D_MODEL), lambda i: (layer, 0, 0)),
          pl.BlockSpec((1, D_MODEL), const),
          pl.BlockSpec((1, D_MODEL), const),
      ],
      out_specs=[pl.BlockSpec((tm, D_MODEL), rowmap)] * n_out,
      out_shape=[jax.ShapeDtypeStruct((n, D_MODEL), F32),
                 jax.ShapeDtypeStruct((n, D_MODEL), BF16)][:n_out],
      compiler_params=pltpu.CompilerParams(
          dimension_semantics=("arbitrary",),
          vmem_limit_bytes=VMEM_LIMIT_BYTES),
      name="out_proj_norm",
  )(o, x, w_out, ln_g.reshape(1, D_MODEL), ln_b.reshape(1, D_MODEL))
  return outs[0], (outs[1] if with_bf16_copy else None)


def _gdn_mixer(x_in, w_all, layer, w_f32, conv_w, a_log, dt_bias, norm_w):
  H = GDN_HEADS
  order = jnp.array(list(range(0, H, 2)) + list(range(1, H, 2)), jnp.int32)
  wat = w_f32[:, GDN_QKV + GDN_VAL + H:].T[order].astype(BF16)
  q, k, v, gate, beta, gcc, gcr = _gdn_in_proj(
      x_in, w_all, layer, wat, conv_w, a_log.reshape(1, H), dt_bias.reshape(1, H),
      a_log[order].reshape(H, 1), dt_bias[order].reshape(H, 1))
  return _gdn_recurrence(q, k, v, gate, beta, gcc, gcr, norm_w)


def kernel(x, gdn_w_in, gdn_conv_w, gdn_a_log, gdn_dt_bias, gdn_norm_w, gdn_w_out,
           gla_w_in, gla_w_gk_up, gla_b_gk, gla_norm_w, gla_w_out, ln_g, ln_b):
  B, T, D = x.shape
  gdn_w_in_b, gdn_w_out_b = gdn_w_in.astype(BF16), gdn_w_out.astype(BF16)
  gla_w_in_b, gla_w_out_b = gla_w_in.astype(BF16), gla_w_out.astype(BF16)
  x_in = x
  x = x.reshape(B * T, D)
  for i in range(DEPTH):
    j = i // 2
    if i % 2 == 0:
      o = _gdn_mixer(x_in, gdn_w_in_b, j, gdn_w_in[j], gdn_conv_w[j], gdn_a_log[j],
                     gdn_dt_bias[j], gdn_norm_w[j])
      w_out = gdn_w_out_b
    else:
      o = _gla_mixer_call(x_in, gla_w_in_b, j, gla_w_gk_up[j].astype(BF16),
                          gla_b_gk[j].reshape(1, GLA_KEY), gla_norm_w[j])
      w_out = gla_w_out_b
    x, xb = _out_proj_norm(o.reshape(B * T, -1), x, w_out, j, ln_g[i], ln_b[i],
                           with_bf16_copy=i + 1 < DEPTH)
    if xb is not None:
      x_in = xb.reshape(B, T, D)
  return x.reshape(B, T, D)
```

```python
import jax
import jax.numpy as jnp
from jax import lax
from jax.experimental import pallas as pl
from jax.experimental.pallas import tpu as pltpu

F32 = jnp.float32
BF16 = jnp.bfloat16

D_MODEL = 1024
DEPTH = 4
CHUNK = 64
CHUNK_SHIFT = 6
CONV_K = 4

GDN_HEADS = 8
GDN_DK = 128
GDN_DV = 256
GDN_KEY = GDN_HEADS * GDN_DK
GDN_VAL = GDN_HEADS * GDN_DV
GDN_QKV = 2 * GDN_KEY + GDN_VAL
GDN_PAIRS = GDN_HEADS // 2

GLA_HEADS = 4
GLA_DK = 128
GLA_DV = 256
GLA_KEY = GLA_HEADS * GLA_DK
GLA_VAL = GLA_HEADS * GLA_DV
GLA_QKV = 2 * GLA_KEY + GLA_VAL
GLA_RANK = 16
GLA_GATE_NORM = 16.0

DEEP_ALPHA = (2.0 * DEPTH) ** 0.25
LN_EPS = 1e-5
RMS_EPS = 1e-6
L2_EPS = 1e-6
LOG2E = 1.4426950408889634

V7X_MXU_WIDTH = 256
V7X_SUBLANES = 8
V7X_LANES = 128
VMEM_LIMIT_BYTES = 56 * 1024 * 1024

IN_ROWS = 256
REC_ROWS = 512
OUT_ROWS = 1024
OUT_BLOCK = 256
COL_TILE = V7X_MXU_WIDTH
SUB = 16
NEG_BIG = -1e30


def _mm(a, b):
  return jnp.dot(a.astype(BF16), b.astype(BF16), preferred_element_type=F32)


def _mm_nt(a, b):
  return lax.dot_general(a.astype(BF16), b.astype(BF16),
                         (((1,), (1,)), ((), ())), preferred_element_type=F32)


def _mm_tn(a, b):
  return lax.dot_general(a.astype(BF16), b.astype(BF16),
                         (((0,), (0,)), ((), ())), preferred_element_type=F32)


def _split3(x):
  hi = x.astype(BF16)
  r1 = x - hi.astype(F32)
  mid = r1.astype(BF16)
  lo = (r1 - mid.astype(F32)).astype(BF16)
  return hi, mid, lo


def _sigmoid(x):
  return 1.0 / (1.0 + jnp.exp(-x))


def _softplus(x):
  return jnp.maximum(x, 0.0) + jnp.log1p(jnp.exp(-jnp.abs(x)))


def _silu(x):
  return x * _sigmoid(x)


def _chunk_tri(n, lower):
  r = lax.broadcasted_iota(jnp.int32, (n, n), 0)
  c = lax.broadcasted_iota(jnp.int32, (n, n), 1)
  same = jnp.right_shift(r, CHUNK_SHIFT) == jnp.right_shift(c, CHUNK_SHIFT)
  tri = (c <= r) if lower else (r <= c)
  return jnp.where(same & tri, 1.0, 0.0).astype(BF16)


def _gated_rmsnorm(o, gate, nw):
  ms = jnp.mean(o * o, axis=-1, keepdims=True)
  return o * lax.rsqrt(ms + RMS_EPS) * nw * _silu(gate)


def _gdn_in_kernel(x_ref, w_ref, wt_ref, wat_ref, convw_ref, alog_ref, dtb_ref, alogt_ref, dtbt_ref,
                   q_ref, k_ref, v_ref, gate_ref, beta_ref, gcc_ref, gcr_ref, pbuf):
  tm = IN_ROWS
  halo = V7X_SUBLANES
  H = GDN_HEADS

  @pl.when(pl.program_id(1) == 0)
  def _():
    pbuf[0:halo, :] = jnp.zeros((halo, GDN_QKV), F32)

  xb = x_ref[0].astype(BF16)

  def gate_cols(c):
    cs = slice(c * COL_TILE, (c + 1) * COL_TILE)
    ws = slice(GDN_QKV + c * COL_TILE, GDN_QKV + (c + 1) * COL_TILE)
    gate_ref[0, :, cs] = jnp.dot(xb, w_ref[0, :, ws], preferred_element_type=F32).astype(BF16)

  n_cols = GDN_QKV // COL_TILE
  proj = lambda c: jnp.dot(xb, w_ref[0, :, c * COL_TILE:(c + 1) * COL_TILE],
                           preferred_element_type=F32)
  acc_next = proj(0)
  for c in range(n_cols):
    cs = slice(c * COL_TILE, (c + 1) * COL_TILE)
    acc = acc_next
    pbuf[halo:halo + tm, cs] = acc
    if c + 1 < n_cols:
      acc_next = proj(c + 1)
    cw = convw_ref[:, cs]
    y = acc * cw[CONV_K - 1:CONV_K]
    for j in range(CONV_K - 1):
      off = halo - (CONV_K - 1) + j
      y = y + pbuf[off:off + tm, cs] * cw[j:j + 1]
    y = _silu(y)
    col0 = c * COL_TILE
    if col0 < 2 * GDN_KEY:
      parts = []
      for j in range(COL_TILE // GDN_DK):
        t = y[:, j * GDN_DK:(j + 1) * GDN_DK]
        parts.append(t * lax.rsqrt(jnp.sum(t * t, axis=-1, keepdims=True) + L2_EPS))
      y = jnp.concatenate(parts, axis=1)
      if col0 < GDN_KEY:
        q_ref[0, :, cs] = (y * (GDN_DK ** -0.5)).astype(BF16)
      else:
        k_ref[0, :, col0 - GDN_KEY:col0 - GDN_KEY + COL_TILE] = y.astype(BF16)
    else:
      v_ref[0, :, col0 - 2 * GDN_KEY:col0 - 2 * GDN_KEY + COL_TILE] = y.astype(BF16)
    if c % 2 == 1:
      gate_cols(c // 2)
  pbuf[0:halo, :] = pbuf[tm:tm + halo, :]

  ba = jnp.dot(xb, wt_ref[0], preferred_element_type=F32)
  beta_ref[0] = _sigmoid(ba[:, :H])
  g = -jnp.exp(alog_ref[...]) * _softplus(ba[:, H:] + dtb_ref[...])
  lmat = _chunk_tri(tm, lower=True)
  gc = jnp.zeros((tm, H), F32)
  for part in _split3(g):
    gc = gc + jnp.dot(lmat, part, preferred_element_type=F32)
  gcc_ref[0] = gc

  at = lax.dot_general(wat_ref[...], xb, (((1,), (1,)), ((), ())),
                       preferred_element_type=F32)
  gt = -jnp.exp(alogt_ref[...]) * _softplus(at + dtbt_ref[...])
  umat = _chunk_tri(tm, lower=False)
  gct = jnp.zeros((H, tm), F32)
  for part in _split3(gt):
    gct = gct + jnp.dot(part, umat, preferred_element_type=F32)
  even = gct[:GDN_PAIRS]
  odd = gct[GDN_PAIRS:]
  odd_r = pltpu.roll(gct, CHUNK, 1)[GDN_PAIRS:]
  even_r = pltpu.roll(gct, tm - CHUNK, 1)[:GDN_PAIRS]
  low_half = (lax.broadcasted_iota(jnp.int32, (GDN_PAIRS, tm), 1) & (V7X_LANES - 1)) < CHUNK
  p_even = jnp.where(low_half, even, odd_r)
  p_odd = jnp.where(low_half, even_r, odd)
  for j in range(tm // V7X_LANES):
    ls = slice(j * V7X_LANES, (j + 1) * V7X_LANES)
    gcr_ref[0, 2 * j] = p_even[:, ls]
    gcr_ref[0, 2 * j + 1] = p_odd[:, ls]


def _gdn_in_proj(xb, w, wt, layer, wat, convw, alog, dtb, alogt, dtbt):
  B, T, D = xb.shape
  tm = IN_ROWS
  H = GDN_HEADS
  const = lambda b, t: (0, 0)
  rowmap = lambda b, t: (b, t, 0)
  return pl.pallas_call(
      _gdn_in_kernel,
      grid=(B, T // tm),
      in_specs=[
          pl.BlockSpec((1, tm, D), rowmap),
          pl.BlockSpec((1,) + w.shape[1:], lambda b, t: (layer, 0, 0)),
          pl.BlockSpec((1,) + wt.shape[1:], lambda b, t: (layer, 0, 0)),
          pl.BlockSpec((H, D), const),
          pl.BlockSpec((CONV_K, GDN_QKV), const),
          pl.BlockSpec((1, H), const),
          pl.BlockSpec((1, H), const),
          pl.BlockSpec((H, 1), const),
          pl.BlockSpec((H, 1), const),
      ],
      out_specs=[
          pl.BlockSpec((1, tm, GDN_KEY), rowmap),
          pl.BlockSpec((1, tm, GDN_KEY), rowmap),
          pl.BlockSpec((1, tm, GDN_VAL), rowmap),
          pl.BlockSpec((1, tm, GDN_VAL), rowmap),
          pl.BlockSpec((1, tm, H), rowmap),
          pl.BlockSpec((1, tm, H), rowmap),
          pl.BlockSpec((1, tm // CHUNK, GDN_PAIRS, V7X_LANES), lambda b, t: (b, t, 0, 0)),
      ],
      out_shape=[
          jax.ShapeDtypeStruct((B, T, GDN_KEY), BF16),
          jax.ShapeDtypeStruct((B, T, GDN_KEY), BF16),
          jax.ShapeDtypeStruct((B, T, GDN_VAL), BF16),
          jax.ShapeDtypeStruct((B, T, GDN_VAL), BF16),
          jax.ShapeDtypeStruct((B, T, H), F32),
          jax.ShapeDtypeStruct((B, T, H), F32),
          jax.ShapeDtypeStruct((B, T // CHUNK, GDN_PAIRS, V7X_LANES), F32),
      ],
      scratch_shapes=[pltpu.VMEM((tm + V7X_SUBLANES, GDN_QKV), F32)],
      compiler_params=pltpu.CompilerParams(
          dimension_semantics=("arbitrary", "arbitrary"),
          vmem_limit_bytes=VMEM_LIMIT_BYTES),
      name="gdn_in_proj",
  )(xb, w, wt, wat, convw, alog, dtb, alogt, dtbt)


def _block_diag(y, bdmask):
  n = y.shape[1] // CHUNK
  return jnp.where(bdmask, jnp.concatenate([y] * n, axis=0), jnp.zeros((), BF16))


def _packed_inverse_minus_eye(a_list, row, colp, bdmask, out):
  same16 = jnp.right_shift(row, 4) == jnp.right_shift(colp, 4)
  same32 = jnp.right_shift(row, 5) == jnp.right_shift(colp, 5)
  bd = lambda xs: [_block_diag(x.astype(BF16), bdmask) for x in xs]
  mm = lambda xs, ybs: [jnp.dot(x.astype(BF16), yb, preferred_element_type=F32)
                        for x, yb in zip(xs, ybs)]
  stack = lambda xs, ys: [jnp.concatenate([x, y], axis=0) for x, y in zip(xs, ys)]
  n = [jnp.where(same16, -a, 0.0) for a in a_list]
  s = mm(n, bd(n))
  yield
  p = n
  for _ in range(2):
    both = mm(stack(p, s), bd(s))
    yield
    p = [x + y + z[:CHUNK] for x, y, z in zip(p, s, both)]
    s = [z[CHUNK:] for z in both]
  ps = mm(p, bd(s))
  yield
  p = [x + y + z for x, y, z in zip(p, s, ps)]
  for mask in (same32 & jnp.logical_not(same16), jnp.logical_not(same32)):
    e = [jnp.where(mask, a, 0.0) for a in a_list]
    pe = mm(p, bd(e))
    yield
    f = [x + y for x, y in zip(e, pe)]
    fp = mm(f, bd(p))
    yield
    p = [x - y - z for x, y, z in zip(p, f, fp)]
  out.extend(p)


def _split_rows(x, low_half):
  return jnp.concatenate([jnp.where(low_half, x, 0.0), jnp.where(low_half, 0.0, x)], axis=0)


def _interleave(*gens):
  gens = list(gens)
  while gens:
    for g in list(gens):
      try:
        next(g)
      except StopIteration:
        gens.remove(g)


def _gdn_rec_kernel(q_ref, k_ref, v_ref, gate_ref, beta_ref, gcc_ref, gcr_ref,
                    nw_ref, o_ref, s_ref, uw_ref, qk_ref):
  @pl.when(pl.program_id(1) == 0)
  def _():
    s_ref[...] = jnp.zeros(s_ref.shape, F32)

  n_chunks = REC_ROWS // CHUNK
  heads = range(GDN_HEADS)
  nw = nw_ref[...]
  row = lax.broadcasted_iota(jnp.int32, (CHUNK, V7X_LANES), 0)
  lane = lax.broadcasted_iota(jnp.int32, (CHUNK, V7X_LANES), 1)
  colp = lane & (CHUNK - 1)
  low_half = lane < CHUNK
  incl = row >= colp
  strict = row > colp
  bd_r = lax.broadcasted_iota(jnp.int32, (2 * CHUNK, V7X_LANES), 0)
  bd_l = lax.broadcasted_iota(jnp.int32, (2 * CHUNK, V7X_LANES), 1)
  bdmask = (bd_r < CHUNK) == (bd_l < CHUNK)
  kbd_r = lax.broadcasted_iota(jnp.int32, (2 * CHUNK, 2 * GDN_DK), 0)
  kbd_l = lax.broadcasted_iota(jnp.int32, (2 * CHUNK, 2 * GDN_DK), 1)
  kbdmask = (kbd_r < CHUNK) == (kbd_l < GDN_DK)
  first_head_k = lax.broadcasted_iota(jnp.int32, (CHUNK, 2 * GDN_DK), 1) < GDN_DK

  rows_of = lambda c: slice(c * CHUNK, (c + 1) * CHUNK)
  klanes = lambda h: slice(h * GDN_DK, (h + 1) * GDN_DK)
  vlanes = lambda h: slice(h * GDN_DV, (h + 1) * GDN_DV)
  col_of = lambda ref, c, h: ref[0, rows_of(c), h:h + 1]

  def phase_a(chunks):
    probs = [(c, p) for c in chunks for p in range(GDN_PAIRS)]
    kb_list, a_list = [], []
    for c, p in probs:
      h0, h1 = 2 * p, 2 * p + 1
      pl2 = slice(h0 * GDN_DK, (h1 + 1) * GDN_DK)
      kp = k_ref[0, rows_of(c), pl2]
      beta_k = jnp.where(first_head_k, col_of(beta_ref, c, h0), col_of(beta_ref, c, h1))
      kb = kp.astype(F32) * beta_k
      lhs = jnp.concatenate([kb.astype(BF16), q_ref[0, rows_of(c), pl2]], axis=0)
      kbd = jnp.where(kbdmask, jnp.concatenate([kp, kp], axis=0), jnp.zeros((), BF16))
      aq = lax.dot_general(lhs, kbd, (((1,), (1,)), ((), ())), preferred_element_type=F32)
      gccp = jnp.where(low_half, col_of(gcc_ref, c, h0), col_of(gcc_ref, c, h1))
      gcrp = gcr_ref[0, c, p:p + 1, :]
      decay = jnp.where(incl, jnp.exp(jnp.where(incl, gccp - gcrp, 0.0)), 0.0)
      a_list.append(jnp.where(strict, aq[:CHUNK] * decay, 0.0))
      qk_ref[c, p] = aq[CHUNK:] * decay
      kb_list.append(kb)
    yield
    c_list = []
    yield from _packed_inverse_minus_eye(a_list, row, colp, bdmask, c_list)
    for (c, p), kb, cm in zip(probs, kb_list, c_list):
      rhs = []
      for j, h in enumerate((2 * p, 2 * p + 1)):
        beta_h = col_of(beta_ref, c, h)
        eg_h = jnp.exp(col_of(gcc_ref, c, h))
        rhs.append(jnp.concatenate(
            [v_ref[0, rows_of(c), vlanes(h)].astype(F32) * beta_h,
             kb[:, j * GDN_DK:(j + 1) * GDN_DK] * eg_h], axis=1))
      rhs = jnp.concatenate(rhs, axis=0)
      uw = rhs + _mm(_split_rows(cm, low_half), rhs)
      uw_ref[c, 2 * p] = uw[:CHUNK]
      uw_ref[c, 2 * p + 1] = uw[CHUNK:]
    yield

  def phase_b(chunks):
    for c in chunks:
      gcc = [col_of(gcc_ref, c, h) for h in heads]
      g_last = [x[CHUNK - 1:CHUNK, :] for x in gcc]
      r = []
      for h in heads:
        qd = q_ref[0, rows_of(c), klanes(h)].astype(F32) * jnp.exp(gcc[h])
        r.append(_mm(jnp.concatenate([uw_ref[c, h, :, GDN_DV:], qd], axis=0), s_ref[h]))
      yield
      v_new = [uw_ref[c, h, :, :GDN_DV] - r[h][:CHUNK] for h in heads]
      o = [None] * GDN_HEADS
      for p in range(GDN_PAIRS):
        h0, h1 = 2 * p, 2 * p + 1
        res = _mm(_split_rows(qk_ref[c, p], low_half),
                  jnp.concatenate([v_new[h0], v_new[h1]], axis=0))
        o[h0] = r[h0][CHUNK:] + res[:CHUNK]
        o[h1] = r[h1][CHUNK:] + res[CHUNK:]
      for h in heads:
        k_dec = k_ref[0, rows_of(c), klanes(h)].astype(F32) * jnp.exp(g_last[h] - gcc[h])
        s_ref[h] = s_ref[h] * jnp.exp(g_last[h]) + _mm_tn(k_dec, v_new[h])
      for h in heads:
        gate = gate_ref[0, rows_of(c), vlanes(h)].astype(F32)
        o_ref[0, rows_of(c), vlanes(h)] = _gated_rmsnorm(o[h], gate, nw).astype(BF16)
      yield

  first, second = range(n_chunks // 2), range(n_chunks // 2, n_chunks)
  _interleave(phase_a(first))
  _interleave(phase_a(second), phase_b(first))
  _interleave(phase_b(second))


def _gdn_recurrence(q, k, v, gate, beta, gcc, gcr, norm_w):
  B, T, _ = q.shape
  tt = REC_ROWS
  H = GDN_HEADS
  rowmap = lambda b, t: (b, t, 0)
  return pl.pallas_call(
      _gdn_rec_kernel,
      grid=(B, T // tt),
      in_specs=[
          pl.BlockSpec((1, tt, GDN_KEY), rowmap),
          pl.BlockSpec((1, tt, GDN_KEY), rowmap),
          pl.BlockSpec((1, tt, GDN_VAL), rowmap),
          pl.BlockSpec((1, tt, GDN_VAL), rowmap),
          pl.BlockSpec((1, tt, H), rowmap),
          pl.BlockSpec((1, tt, H), rowmap),
          pl.BlockSpec((1, tt // CHUNK, GDN_PAIRS, V7X_LANES), lambda b, t: (b, t, 0, 0)),
          pl.BlockSpec((1, GDN_DV), lambda b, t: (0, 0)),
      ],
      out_specs=pl.BlockSpec((1, tt, GDN_VAL), rowmap),
      out_shape=jax.ShapeDtypeStruct((B, T, GDN_VAL), BF16),
      scratch_shapes=[
          pltpu.VMEM((H, GDN_DK, GDN_DV), F32),
          pltpu.VMEM((tt // CHUNK, H, CHUNK, GDN_DV + GDN_DK), F32),
          pltpu.VMEM((tt // CHUNK, GDN_PAIRS, CHUNK, V7X_LANES), F32),
      ],
      compiler_params=pltpu.CompilerParams(
          dimension_semantics=("arbitrary", "arbitrary"),
          vmem_limit_bytes=VMEM_LIMIT_BYTES),
      name="gdn_recurrence",
  )(q, k, v, gate, beta, gcc, gcr, norm_w.reshape(1, GDN_DV))


GLA_ROWS = 256
GLA_Q0, GLA_K0, GLA_V0, GLA_G0 = 0, GLA_KEY, 2 * GLA_KEY, GLA_QKV


def _gla_chunk_all_heads(c, pv, bv, nw, o_ref, st_ref, b2_ref, kf_ref):
  heads = range(GLA_HEADS)
  nsub = CHUNK // SUB
  half = V7X_SUBLANES
  rows = slice(c * CHUNK, (c + 1) * CHUNK)
  ks = [slice(h * GLA_DK, (h + 1) * GLA_DK) for h in heads]
  vs = [slice(h * GLA_DV, (h + 1) * GLA_DV) for h in heads]
  lanes = lambda base, sl: slice(base + sl.start, base + sl.stop)
  q = [pv[rows, lanes(GLA_Q0, ks[h])].astype(F32) for h in heads]
  k = [pv[rows, lanes(GLA_K0, ks[h])].astype(F32) for h in heads]
  v = [pv[rows, lanes(GLA_V0, vs[h])] for h in heads]
  b2 = [bv[rows, ks[h]] * LOG2E for h in heads]
  b2_last = [x[CHUNK - 1:CHUNK, :] for x in b2]
  for h in heads:
    b2_ref[h] = b2[h]
    kf_ref[h] = k[h]
  o_inter = [_mm_nt(q[h] * jnp.exp2(b2[h]), st_ref[h]) for h in heads]

  rows8 = lax.broadcasted_iota(jnp.int32, (half, 1), 0)
  lane64 = lax.broadcasted_iota(jnp.int32, (half, CHUNK), 1)
  bands = [[None] * nsub for _ in heads]
  for h in heads:
    for i in range(nsub):
      r0 = i * SUB
      q_lo, q_hi = q[h][r0:r0 + half], q[h][r0 + half:r0 + SUB]
      b_lo, b_hi = b2[h][r0:r0 + half], b2[h][r0 + half:r0 + SUB]
      band_lo = jnp.zeros((half, CHUNK), F32)
      band_hi = jnp.zeros((half, CHUNK), F32)
      for s in range(SUB):
        b_s = b2_ref[h, r0 + s:r0 + s + 1, :]
        k_s = kf_ref[h, r0 + s:r0 + s + 1, :]
        if s < half:
          d = b_lo - b_s
          if s > 0:
            d = d + jnp.where(rows8 >= s, 0.0, NEG_BIG)
          col = jnp.sum(q_lo * k_s * jnp.exp2(d), axis=-1, keepdims=True)
          band_lo = jnp.where(lane64 == r0 + s, col, band_lo)
          d = b_hi - b_s
        else:
          d = b_hi - b_s
          if s > half:
            d = d + jnp.where(rows8 >= s - half, 0.0, NEG_BIG)
        col = jnp.sum(q_hi * k_s * jnp.exp2(d), axis=-1, keepdims=True)
        band_hi = jnp.where(lane64 == r0 + s, col, band_hi)
      bands[h][i] = jnp.concatenate([band_lo, band_hi], axis=0)
  for i in range(1, nsub):
    r0 = i * SUB
    for h in heads:
      ref = b2[h][r0:r0 + 1]
      qh = q[h][r0:r0 + SUB] * jnp.exp2(b2[h][r0:r0 + SUB] - ref)
      kh = jnp.concatenate([k[h][:r0] * jnp.exp2(ref - b2[h][:r0]),
                            jnp.zeros((CHUNK - r0, GLA_DK), F32)], axis=0)
      bands[h][i] = bands[h][i] + _mm_nt(qh, kh)
  o = [o_inter[h] + _mm(jnp.concatenate(bands[h], axis=0), v[h]) for h in heads]
  for h in heads:
    st_ref[h] = (st_ref[h] * jnp.exp2(b2_last[h])
                 + _mm_tn(v[h], k[h] * jnp.exp2(b2_last[h] - b2[h])))
  for h in heads:
    gate = pv[rows, lanes(GLA_G0, vs[h])].astype(F32)
    o_ref[0, rows, vs[h]] = _gated_rmsnorm(o[h], gate, nw).astype(BF16)


def _gla_mixer_kernel(x_ref, w_ref, wt_ref, wup_ref, bgk_ref, nw_ref, o_ref,
                      proj_ref, b_ref, st_ref, b2_ref, kf_ref):
  tm = GLA_ROWS
  n_chunks = tm // CHUNK
  piece = (GLA_QKV + GLA_VAL) // n_chunks

  @pl.when(pl.program_id(1) == 0)
  def _():
    st_ref[...] = jnp.zeros(st_ref.shape, F32)
    proj_ref[...] = jnp.zeros(proj_ref.shape, BF16)
    b_ref[...] = jnp.zeros(b_ref.shape, F32)

  cur = lax.rem(pl.program_id(1), 2)
  prev = 1 - cur
  pv_prev, bv_prev = proj_ref.at[prev], b_ref.at[prev]
  pv_cur, bv_cur = proj_ref.at[cur], b_ref.at[cur]
  nw = nw_ref[...]
  xb = x_ref[0]

  low = jnp.dot(xb, wt_ref[0], preferred_element_type=F32)
  lmat = _chunk_tri(CHUNK, lower=True)
  for c in range(n_chunks):
    rows = slice(c * CHUNK, (c + 1) * CHUNK)
    cols = slice(c * piece, (c + 1) * piece)
    acc = jnp.dot(xb, w_ref[0, :, cols], preferred_element_type=F32)
    logit = _mm(low[rows], wup_ref[...]) + bgk_ref[...]
    gk = (jnp.minimum(logit, 0.0) - jnp.log1p(jnp.exp(-jnp.abs(logit)))) * (1.0 / GLA_GATE_NORM)
    bsum = jnp.zeros((CHUNK, GLA_KEY), F32)
    for part in _split3(gk):
      bsum = bsum + jnp.dot(lmat, part, preferred_element_type=F32)
    _gla_chunk_all_heads(c, pv_prev, bv_prev, nw, o_ref, st_ref, b2_ref, kf_ref)
    n_q = max(0, min(piece, GLA_KEY - c * piece))
    if n_q == piece:
      acc = acc * (GLA_DK ** -0.5)
    elif n_q:
      acc = jnp.concatenate([acc[:, :n_q] * (GLA_DK ** -0.5), acc[:, n_q:]], axis=1)
    pv_cur[:, cols] = acc.astype(BF16)
    bv_cur[rows, :] = bsum


def _gla_mixer_call(xb, w, wt, layer, wup, bgk, norm_w):
  B, T, D = xb.shape
  tm = GLA_ROWS
  n_t = T // tm
  const = lambda b, t: (0, 0)
  return pl.pallas_call(
      _gla_mixer_kernel,
      grid=(B, n_t + 1),
      in_specs=[
          pl.BlockSpec((1, tm, D), lambda b, t: (b, jnp.minimum(t, n_t - 1), 0)),
          pl.BlockSpec((1,) + w.shape[1:], lambda b, t: (layer, 0, 0)),
          pl.BlockSpec((1,) + wt.shape[1:], lambda b, t: (layer, 0, 0)),
          pl.BlockSpec((GLA_RANK, GLA_KEY), const),
          pl.BlockSpec((1, GLA_KEY), const),
          pl.BlockSpec((1, GLA_DV), const),
      ],
      out_specs=pl.BlockSpec((1, tm, GLA_VAL), lambda b, t: (b, jnp.maximum(t - 1, 0), 0)),
      out_shape=jax.ShapeDtypeStruct((B, T, GLA_VAL), BF16),
      scratch_shapes=[
          pltpu.VMEM((2, tm, GLA_QKV + GLA_VAL), BF16),
          pltpu.VMEM((2, tm, GLA_KEY), F32),
          pltpu.VMEM((GLA_HEADS, GLA_DV, GLA_DK), F32),
          pltpu.VMEM((GLA_HEADS, CHUNK, GLA_DK), F32),
          pltpu.VMEM((GLA_HEADS, CHUNK, GLA_DK), F32),
      ],
      compiler_params=pltpu.CompilerParams(
          dimension_semantics=("arbitrary", "arbitrary"),
          vmem_limit_bytes=VMEM_LIMIT_BYTES),
      name="gla_mixer",
  )(xb, w, wt, wup, bgk, norm_w.reshape(1, GLA_DV))


def _out_kernel(o_ref, x_ref, w_ref, g_ref, b_ref, y_ref, *maybe_yb_ref):
  nblk = OUT_ROWS // OUT_BLOCK
  rows_of = lambda i: slice(i * OUT_BLOCK, (i + 1) * OUT_BLOCK)
  proj = lambda i: jnp.dot(o_ref[rows_of(i), :], w_ref[0], preferred_element_type=F32)
  y = proj(0)
  for i in range(nblk):
    y_next = proj(i + 1) if i + 1 < nblk else None
    z = DEEP_ALPHA * x_ref[rows_of(i), :] + y
    mu = jnp.mean(z, axis=-1, keepdims=True)
    zc = z - mu
    var = jnp.mean(zc * zc, axis=-1, keepdims=True)
    out = zc * lax.rsqrt(var + LN_EPS) * g_ref[...] + b_ref[...]
    y_ref[rows_of(i), :] = out
    for yb_ref in maybe_yb_ref:
      yb_ref[rows_of(i), :] = out.astype(BF16)
    y = y_next


def _out_proj_norm(o, x, w_out, layer, ln_g, ln_b, with_bf16_copy):
  n, dv = o.shape
  tm = OUT_ROWS
  const = lambda i: (0, 0)
  rowmap = lambda i: (i, 0)
  n_out = 2 if with_bf16_copy else 1
  outs = pl.pallas_call(
      _out_kernel,
      grid=(n // tm,),
      in_specs=[
          pl.BlockSpec((tm, dv), rowmap),
          pl.BlockSpec((tm, D_MODEL), rowmap),
          pl.BlockSpec((1, dv, D_MODEL), lambda i: (layer, 0, 0)),
          pl.BlockSpec((1, D_MODEL), const),
          pl.BlockSpec((1, D_MODEL), const),
      ],
      out_specs=[pl.BlockSpec((tm, D_MODEL), rowmap)] * n_out,
      out_shape=[jax.ShapeDtypeStruct((n, D_MODEL), F32),
                 jax.ShapeDtypeStruct((n, D_MODEL), BF16)][:n_out],
      compiler_params=pltpu.CompilerParams(
          dimension_semantics=("arbitrary",),
          vmem_limit_bytes=VMEM_LIMIT_BYTES),
      name="out_proj_norm",
  )(o, x, w_out, ln_g.reshape(1, D_MODEL), ln_b.reshape(1, D_MODEL))
  return outs[0], (outs[1] if with_bf16_copy else None)


def _gdn_mixer(x_in, w_main, w_tail, layer, w_f32, conv_w, a_log, dt_bias, norm_w):
  H = GDN_HEADS
  order = jnp.array(list(range(0, H, 2)) + list(range(1, H, 2)), jnp.int32)
  wat = w_f32[:, GDN_QKV + GDN_VAL + H:].T[order].astype(BF16)
  q, k, v, gate, beta, gcc, gcr = _gdn_in_proj(
      x_in, w_main, w_tail, layer, wat, conv_w, a_log.reshape(1, H), dt_bias.reshape(1, H),
      a_log[order].reshape(H, 1), dt_bias[order].reshape(H, 1))
  return _gdn_recurrence(q, k, v, gate, beta, gcc, gcr, norm_w)


def kernel(x, gdn_w_in, gdn_conv_w, gdn_a_log, gdn_dt_bias, gdn_norm_w, gdn_w_out,
           gla_w_in, gla_w_gk_up, gla_b_gk, gla_norm_w, gla_w_out, ln_g, ln_b):
  B, T, D = x.shape
  n_gdn, n_gla = GDN_QKV + GDN_VAL, GLA_QKV + GLA_VAL
  gdn_w_main, gdn_w_tail = gdn_w_in[:, :, :n_gdn].astype(BF16), gdn_w_in[:, :, n_gdn:].astype(BF16)
  gla_w_main, gla_w_tail = gla_w_in[:, :, :n_gla].astype(BF16), gla_w_in[:, :, n_gla:].astype(BF16)
  gdn_w_out_b, gla_w_out_b = gdn_w_out.astype(BF16), gla_w_out.astype(BF16)
  x_in = x
  x = x.reshape(B * T, D)
  for i in range(DEPTH):
    j = i // 2
    if i % 2 == 0:
      o = _gdn_mixer(x_in, gdn_w_main, gdn_w_tail, j, gdn_w_in[j], gdn_conv_w[j], gdn_a_log[j],
                     gdn_dt_bias[j], gdn_norm_w[j])
      w_out = gdn_w_out_b
    else:
      o = _gla_mixer_call(x_in, gla_w_main, gla_w_tail, j, gla_w_gk_up[j].astype(BF16),
                          gla_b_gk[j].reshape(1, GLA_KEY), gla_norm_w[j])
      w_out = gla_w_out_b
    x, xb = _out_proj_norm(o.reshape(B * T, -1), x, w_out, j, ln_g[i], ln_b[i],
                           with_bf16_copy=i + 1 < DEPTH)
    if xb is not None:
      x_in = xb.reshape(B, T, D)
  return x.reshape(B, T, D)
```

```python
import jax
import jax.numpy as jnp
from jax import lax
from jax.experimental import pallas as pl
from jax.experimental.pallas import tpu as pltpu

F32 = jnp.float32
BF16 = jnp.bfloat16

D_MODEL = 1024
DEPTH = 4
CHUNK = 64
CHUNK_SHIFT = 6
CONV_K = 4

GDN_HEADS = 8
GDN_DK = 128
GDN_DV = 256
GDN_KEY = GDN_HEADS * GDN_DK
GDN_VAL = GDN_HEADS * GDN_DV
GDN_QKV = 2 * GDN_KEY + GDN_VAL
GDN_PAIRS = GDN_HEADS // 2

GLA_HEADS = 4
GLA_DK = 128
GLA_DV = 256
GLA_KEY = GLA_HEADS * GLA_DK
GLA_VAL = GLA_HEADS * GLA_DV
GLA_QKV = 2 * GLA_KEY + GLA_VAL
GLA_RANK = 16
GLA_GATE_NORM = 16.0

DEEP_ALPHA = (2.0 * DEPTH) ** 0.25
LN_EPS = 1e-5
RMS_EPS = 1e-6
L2_EPS = 1e-6
LOG2E = 1.4426950408889634

V7X_MXU_WIDTH = 256
V7X_SUBLANES = 8
V7X_LANES = 128
VMEM_LIMIT_BYTES = 56 * 1024 * 1024

IN_ROWS = 512
REC_ROWS = 512
OUT_ROWS = 1024
OUT_BLOCK = 256
CAST_ROWS = 256
COL_TILE = V7X_MXU_WIDTH
SUB = 16
NEG_BIG = -1e30


def _mm(a, b):
  return jnp.dot(a.astype(BF16), b.astype(BF16), preferred_element_type=F32)


def _mm_nt(a, b):
  return lax.dot_general(a.astype(BF16), b.astype(BF16),
                         (((1,), (1,)), ((), ())), preferred_element_type=F32)


def _mm_tn(a, b):
  return lax.dot_general(a.astype(BF16), b.astype(BF16),
                         (((0,), (0,)), ((), ())), preferred_element_type=F32)


def _split3(x):
  hi = x.astype(BF16)
  r1 = x - hi.astype(F32)
  mid = r1.astype(BF16)
  lo = (r1 - mid.astype(F32)).astype(BF16)
  return hi, mid, lo


def _sigmoid(x):
  return 1.0 / (1.0 + jnp.exp(-x))


def _softplus(x):
  return jnp.maximum(x, 0.0) + jnp.log1p(jnp.exp(-jnp.abs(x)))


def _silu(x):
  return x * _sigmoid(x)


def _chunk_tri(n, lower):
  r = lax.broadcasted_iota(jnp.int32, (n, n), 0)
  c = lax.broadcasted_iota(jnp.int32, (n, n), 1)
  same = jnp.right_shift(r, CHUNK_SHIFT) == jnp.right_shift(c, CHUNK_SHIFT)
  tri = (c <= r) if lower else (r <= c)
  return jnp.where(same & tri, 1.0, 0.0).astype(BF16)


def _gated_rmsnorm(o, gate, nw):
  ms = jnp.mean(o * o, axis=-1, keepdims=True)
  return o * lax.rsqrt(ms + RMS_EPS) * nw * _silu(gate)


def _gdn_in_kernel(x_ref, w_ref, wat_ref, convw_ref, alog_ref, dtb_ref, alogt_ref, dtbt_ref,
                   q_ref, k_ref, v_ref, gate_ref, beta_ref, gcc_ref, gcr_ref, pbuf):
  tm = IN_ROWS
  halo = V7X_SUBLANES
  H = GDN_HEADS

  @pl.when(pl.program_id(1) == 0)
  def _():
    pbuf[0:halo, :] = jnp.zeros((halo, GDN_QKV), F32)

  xb = x_ref[0].astype(BF16)

  def gate_cols(c):
    cs = slice(c * COL_TILE, (c + 1) * COL_TILE)
    ws = slice(GDN_QKV + c * COL_TILE, GDN_QKV + (c + 1) * COL_TILE)
    gate_ref[0, :, cs] = jnp.dot(xb, w_ref[0, :, ws], preferred_element_type=F32).astype(BF16)

  n_cols = GDN_QKV // COL_TILE
  proj = lambda c: jnp.dot(xb, w_ref[0, :, c * COL_TILE:(c + 1) * COL_TILE],
                           preferred_element_type=F32)
  acc_next = proj(0)
  for c in range(n_cols):
    cs = slice(c * COL_TILE, (c + 1) * COL_TILE)
    acc = acc_next
    pbuf[halo:halo + tm, cs] = acc
    if c + 1 < n_cols:
      acc_next = proj(c + 1)
    cw = convw_ref[:, cs]
    y = acc * cw[CONV_K - 1:CONV_K]
    for j in range(CONV_K - 1):
      off = halo - (CONV_K - 1) + j
      y = y + pbuf[off:off + tm, cs] * cw[j:j + 1]
    y = _silu(y)
    col0 = c * COL_TILE
    if col0 < 2 * GDN_KEY:
      parts = []
      for j in range(COL_TILE // GDN_DK):
        t = y[:, j * GDN_DK:(j + 1) * GDN_DK]
        parts.append(t * lax.rsqrt(jnp.sum(t * t, axis=-1, keepdims=True) + L2_EPS))
      y = jnp.concatenate(parts, axis=1)
      if col0 < GDN_KEY:
        q_ref[0, :, cs] = (y * (GDN_DK ** -0.5)).astype(BF16)
      else:
        k_ref[0, :, col0 - GDN_KEY:col0 - GDN_KEY + COL_TILE] = y.astype(BF16)
    else:
      v_ref[0, :, col0 - 2 * GDN_KEY:col0 - 2 * GDN_KEY + COL_TILE] = y.astype(BF16)
    if c % 2 == 1:
      gate_cols(c // 2)
  pbuf[0:halo, :] = pbuf[tm:tm + halo, :]

  ba = jnp.dot(xb, w_ref[0, :, GDN_QKV + GDN_VAL:], preferred_element_type=F32)
  beta_ref[0] = _sigmoid(ba[:, :H])
  g = -jnp.exp(alog_ref[...]) * _softplus(ba[:, H:] + dtb_ref[...])
  lmat = _chunk_tri(tm, lower=True)
  gc = jnp.zeros((tm, H), F32)
  for part in _split3(g):
    gc = gc + jnp.dot(lmat, part, preferred_element_type=F32)
  gcc_ref[0] = gc

  at = lax.dot_general(wat_ref[...], xb, (((1,), (1,)), ((), ())),
                       preferred_element_type=F32)
  gt = -jnp.exp(alogt_ref[...]) * _softplus(at + dtbt_ref[...])
  umat = _chunk_tri(tm, lower=False)
  gct = jnp.zeros((H, tm), F32)
  for part in _split3(gt):
    gct = gct + jnp.dot(part, umat, preferred_element_type=F32)
  even = gct[:GDN_PAIRS]
  odd = gct[GDN_PAIRS:]
  odd_r = pltpu.roll(gct, CHUNK, 1)[GDN_PAIRS:]
  even_r = pltpu.roll(gct, tm - CHUNK, 1)[:GDN_PAIRS]
  low_half = (lax.broadcasted_iota(jnp.int32, (GDN_PAIRS, tm), 1) & (V7X_LANES - 1)) < CHUNK
  p_even = jnp.where(low_half, even, odd_r)
  p_odd = jnp.where(low_half, even_r, odd)
  for j in range(tm // V7X_LANES):
    ls = slice(j * V7X_LANES, (j + 1) * V7X_LANES)
    gcr_ref[0, 2 * j] = p_even[:, ls]
    gcr_ref[0, 2 * j + 1] = p_odd[:, ls]


def _gdn_in_proj(xb, w, layer, wat, convw, alog, dtb, alogt, dtbt):
  B, T, D = xb.shape
  tm = IN_ROWS
  H = GDN_HEADS
  const = lambda b, t: (0, 0)
  rowmap = lambda b, t: (b, t, 0)
  return pl.pallas_call(
      _gdn_in_kernel,
      grid=(B, T // tm),
      in_specs=[
          pl.BlockSpec((1, tm, D), rowmap),
          pl.BlockSpec((1,) + w.shape[1:], lambda b, t: (layer, 0, 0)),
          pl.BlockSpec((H, D), const),
          pl.BlockSpec((CONV_K, GDN_QKV), const),
          pl.BlockSpec((1, H), const),
          pl.BlockSpec((1, H), const),
          pl.BlockSpec((H, 1), const),
          pl.BlockSpec((H, 1), const),
      ],
      out_specs=[
          pl.BlockSpec((1, tm, GDN_KEY), rowmap),
          pl.BlockSpec((1, tm, GDN_KEY), rowmap),
          pl.BlockSpec((1, tm, GDN_VAL), rowmap),
          pl.BlockSpec((1, tm, GDN_VAL), rowmap),
          pl.BlockSpec((1, tm, H), rowmap),
          pl.BlockSpec((1, tm, H), rowmap),
          pl.BlockSpec((1, tm // CHUNK, GDN_PAIRS, V7X_LANES), lambda b, t: (b, t, 0, 0)),
      ],
      out_shape=[
          jax.ShapeDtypeStruct((B, T, GDN_KEY), BF16),
          jax.ShapeDtypeStruct((B, T, GDN_KEY), BF16),
          jax.ShapeDtypeStruct((B, T, GDN_VAL), BF16),
          jax.ShapeDtypeStruct((B, T, GDN_VAL), BF16),
          jax.ShapeDtypeStruct((B, T, H), F32),
          jax.ShapeDtypeStruct((B, T, H), F32),
          jax.ShapeDtypeStruct((B, T // CHUNK, GDN_PAIRS, V7X_LANES), F32),
      ],
      scratch_shapes=[pltpu.VMEM((tm + V7X_SUBLANES, GDN_QKV), F32)],
      compiler_params=pltpu.CompilerParams(
          dimension_semantics=("arbitrary", "arbitrary"),
          vmem_limit_bytes=VMEM_LIMIT_BYTES),
      name="gdn_in_proj",
  )(xb, w, wat, convw, alog, dtb, alogt, dtbt)


def _block_diag(y, bdmask):
  n = y.shape[1] // CHUNK
  return jnp.where(bdmask, jnp.concatenate([y] * n, axis=0), jnp.zeros((), BF16))


def _packed_inverse_minus_eye(a_list, row, colp, bdmask, out):
  same16 = jnp.right_shift(row, 4) == jnp.right_shift(colp, 4)
  same32 = jnp.right_shift(row, 5) == jnp.right_shift(colp, 5)
  bd = lambda xs: [_block_diag(x.astype(BF16), bdmask) for x in xs]
  mm = lambda xs, ybs: [jnp.dot(x.astype(BF16), yb, preferred_element_type=F32)
                        for x, yb in zip(xs, ybs)]
  stack = lambda xs, ys: [jnp.concatenate([x, y], axis=0) for x, y in zip(xs, ys)]
  n = [jnp.where(same16, -a, 0.0) for a in a_list]
  s = mm(n, bd(n))
  yield
  p = n
  for _ in range(2):
    both = mm(stack(p, s), bd(s))
    yield
    p = [x + y + z[:CHUNK] for x, y, z in zip(p, s, both)]
    s = [z[CHUNK:] for z in both]
  ps = mm(p, bd(s))
  yield
  p = [x + y + z for x, y, z in zip(p, s, ps)]
  for mask in (same32 & jnp.logical_not(same16), jnp.logical_not(same32)):
    e = [jnp.where(mask, a, 0.0) for a in a_list]
    pe = mm(p, bd(e))
    yield
    f = [x + y for x, y in zip(e, pe)]
    fp = mm(f, bd(p))
    yield
    p = [x - y - z for x, y, z in zip(p, f, fp)]
  out.extend(p)


def _split_rows(x, low_half):
  return jnp.concatenate([jnp.where(low_half, x, 0.0), jnp.where(low_half, 0.0, x)], axis=0)


def _interleave(*gens):
  gens = list(gens)
  while gens:
    for g in list(gens):
      try:
        next(g)
      except StopIteration:
        gens.remove(g)


def _gdn_rec_kernel(q_ref, k_ref, v_ref, gate_ref, beta_ref, gcc_ref, gcr_ref,
                    nw_ref, o_ref, s_ref, uw_ref, qk_ref):
  @pl.when(pl.program_id(1) == 0)
  def _():
    s_ref[...] = jnp.zeros(s_ref.shape, F32)

  n_chunks = REC_ROWS // CHUNK
  heads = range(GDN_HEADS)
  nw = nw_ref[...]
  row = lax.broadcasted_iota(jnp.int32, (CHUNK, V7X_LANES), 0)
  lane = lax.broadcasted_iota(jnp.int32, (CHUNK, V7X_LANES), 1)
  colp = lane & (CHUNK - 1)
  low_half = lane < CHUNK
  incl = row >= colp
  strict = row > colp
  bd_r = lax.broadcasted_iota(jnp.int32, (2 * CHUNK, V7X_LANES), 0)
  bd_l = lax.broadcasted_iota(jnp.int32, (2 * CHUNK, V7X_LANES), 1)
  bdmask = (bd_r < CHUNK) == (bd_l < CHUNK)
  kbd_r = lax.broadcasted_iota(jnp.int32, (2 * CHUNK, 2 * GDN_DK), 0)
  kbd_l = lax.broadcasted_iota(jnp.int32, (2 * CHUNK, 2 * GDN_DK), 1)
  kbdmask = (kbd_r < CHUNK) == (kbd_l < GDN_DK)
  first_head_k = lax.broadcasted_iota(jnp.int32, (CHUNK, 2 * GDN_DK), 1) < GDN_DK

  rows_of = lambda c: slice(c * CHUNK, (c + 1) * CHUNK)
  klanes = lambda h: slice(h * GDN_DK, (h + 1) * GDN_DK)
  vlanes = lambda h: slice(h * GDN_DV, (h + 1) * GDN_DV)
  col_of = lambda ref, c, h: ref[0, rows_of(c), h:h + 1]

  def phase_a(chunks):
    probs = [(c, p) for c in chunks for p in range(GDN_PAIRS)]
    kb_list, a_list = [], []
    for c, p in probs:
      h0, h1 = 2 * p, 2 * p + 1
      pl2 = slice(h0 * GDN_DK, (h1 + 1) * GDN_DK)
      kp = k_ref[0, rows_of(c), pl2]
      beta_k = jnp.where(first_head_k, col_of(beta_ref, c, h0), col_of(beta_ref, c, h1))
      kb = kp.astype(F32) * beta_k
      lhs = jnp.concatenate([kb.astype(BF16), q_ref[0, rows_of(c), pl2]], axis=0)
      kbd = jnp.where(kbdmask, jnp.concatenate([kp, kp], axis=0), jnp.zeros((), BF16))
      aq = lax.dot_general(lhs, kbd, (((1,), (1,)), ((), ())), preferred_element_type=F32)
      gccp = jnp.where(low_half, col_of(gcc_ref, c, h0), col_of(gcc_ref, c, h1))
      gcrp = gcr_ref[0, c, p:p + 1, :]
      decay = jnp.where(incl, jnp.exp(jnp.where(incl, gccp - gcrp, 0.0)), 0.0)
      a_list.append(jnp.where(strict, aq[:CHUNK] * decay, 0.0))
      qk_ref[c, p] = aq[CHUNK:] * decay
      kb_list.append(kb)
    yield
    c_list = []
    yield from _packed_inverse_minus_eye(a_list, row, colp, bdmask, c_list)
    for (c, p), kb, cm in zip(probs, kb_list, c_list):
      rhs = []
      for j, h in enumerate((2 * p, 2 * p + 1)):
        beta_h = col_of(beta_ref, c, h)
        eg_h = jnp.exp(col_of(gcc_ref, c, h))
        rhs.append(jnp.concatenate(
            [v_ref[0, rows_of(c), vlanes(h)].astype(F32) * beta_h,
             kb[:, j * GDN_DK:(j + 1) * GDN_DK] * eg_h], axis=1))
      rhs = jnp.concatenate(rhs, axis=0)
      uw = rhs + _mm(_split_rows(cm, low_half), rhs)
      uw_ref[c, 2 * p] = uw[:CHUNK]
      uw_ref[c, 2 * p + 1] = uw[CHUNK:]
    yield

  def phase_b(chunks):
    for c in chunks:
      gcc = [col_of(gcc_ref, c, h) for h in heads]
      g_last = [x[CHUNK - 1:CHUNK, :] for x in gcc]
      r = []
      for h in heads:
        qd = q_ref[0, rows_of(c), klanes(h)].astype(F32) * jnp.exp(gcc[h])
        r.append(_mm(jnp.concatenate([uw_ref[c, h, :, GDN_DV:], qd], axis=0), s_ref[h]))
      yield
      v_new = [uw_ref[c, h, :, :GDN_DV] - r[h][:CHUNK] for h in heads]
      o = [None] * GDN_HEADS
      for p in range(GDN_PAIRS):
        h0, h1 = 2 * p, 2 * p + 1
        res = _mm(_split_rows(qk_ref[c, p], low_half),
                  jnp.concatenate([v_new[h0], v_new[h1]], axis=0))
        o[h0] = r[h0][CHUNK:] + res[:CHUNK]
        o[h1] = r[h1][CHUNK:] + res[CHUNK:]
      for h in heads:
        k_dec = k_ref[0, rows_of(c), klanes(h)].astype(F32) * jnp.exp(g_last[h] - gcc[h])
        s_ref[h] = s_ref[h] * jnp.exp(g_last[h]) + _mm_tn(k_dec, v_new[h])
      for h in heads:
        gate = gate_ref[0, rows_of(c), vlanes(h)].astype(F32)
        o_ref[0, rows_of(c), vlanes(h)] = _gated_rmsnorm(o[h], gate, nw).astype(BF16)
      yield

  first, second = range(n_chunks // 2), range(n_chunks // 2, n_chunks)
  _interleave(phase_a(first))
  _interleave(phase_a(second), phase_b(first))
  _interleave(phase_b(second))


def _gdn_recurrence(q, k, v, gate, beta, gcc, gcr, norm_w):
  B, T, _ = q.shape
  tt = REC_ROWS
  H = GDN_HEADS
  rowmap = lambda b, t: (b, t, 0)
  return pl.pallas_call(
      _gdn_rec_kernel,
      grid=(B, T // tt),
      in_specs=[
          pl.BlockSpec((1, tt, GDN_KEY), rowmap),
          pl.BlockSpec((1, tt, GDN_KEY), rowmap),
          pl.BlockSpec((1, tt, GDN_VAL), rowmap),
          pl.BlockSpec((1, tt, GDN_VAL), rowmap),
          pl.BlockSpec((1, tt, H), rowmap),
          pl.BlockSpec((1, tt, H), rowmap),
          pl.BlockSpec((1, tt // CHUNK, GDN_PAIRS, V7X_LANES), lambda b, t: (b, t, 0, 0)),
          pl.BlockSpec((1, GDN_DV), lambda b, t: (0, 0)),
      ],
      out_specs=pl.BlockSpec((1, tt, GDN_VAL), rowmap),
      out_shape=jax.ShapeDtypeStruct((B, T, GDN_VAL), BF16),
      scratch_shapes=[
          pltpu.VMEM((H, GDN_DK, GDN_DV), F32),
          pltpu.VMEM((tt // CHUNK, H, CHUNK, GDN_DV + GDN_DK), F32),
          pltpu.VMEM((tt // CHUNK, GDN_PAIRS, CHUNK, V7X_LANES), F32),
      ],
      compiler_params=pltpu.CompilerParams(
          dimension_semantics=("arbitrary", "arbitrary"),
          vmem_limit_bytes=VMEM_LIMIT_BYTES),
      name="gdn_recurrence",
  )(q, k, v, gate, beta, gcc, gcr, norm_w.reshape(1, GDN_DV))


GLA_ROWS = 256
GLA_Q0, GLA_K0, GLA_V0, GLA_G0 = 0, GLA_KEY, 2 * GLA_KEY, GLA_QKV


def _gla_chunk_all_heads(c, pv, bv, nw, o_ref, st_ref, b2_ref, kf_ref):
  heads = range(GLA_HEADS)
  nsub = CHUNK // SUB
  half = V7X_SUBLANES
  rows = slice(c * CHUNK, (c + 1) * CHUNK)
  ks = [slice(h * GLA_DK, (h + 1) * GLA_DK) for h in heads]
  vs = [slice(h * GLA_DV, (h + 1) * GLA_DV) for h in heads]
  lanes = lambda base, sl: slice(base + sl.start, base + sl.stop)
  q = [pv[rows, lanes(GLA_Q0, ks[h])].astype(F32) for h in heads]
  k = [pv[rows, lanes(GLA_K0, ks[h])].astype(F32) for h in heads]
  v = [pv[rows, lanes(GLA_V0, vs[h])] for h in heads]
  b2 = [bv[rows, ks[h]] * LOG2E for h in heads]
  b2_last = [x[CHUNK - 1:CHUNK, :] for x in b2]
  for h in heads:
    b2_ref[h] = b2[h]
    kf_ref[h] = k[h]
  o_inter = [_mm_nt(q[h] * jnp.exp2(b2[h]), st_ref[h]) for h in heads]

  rows8 = lax.broadcasted_iota(jnp.int32, (half, 1), 0)
  lane64 = lax.broadcasted_iota(jnp.int32, (half, CHUNK), 1)
  bands = [[None] * nsub for _ in heads]
  for h in heads:
    for i in range(nsub):
      r0 = i * SUB
      q_lo, q_hi = q[h][r0:r0 + half], q[h][r0 + half:r0 + SUB]
      b_lo, b_hi = b2[h][r0:r0 + half], b2[h][r0 + half:r0 + SUB]
      band_lo = jnp.zeros((half, CHUNK), F32)
      band_hi = jnp.zeros((half, CHUNK), F32)
      for s in range(SUB):
        b_s = b2_ref[h, r0 + s:r0 + s + 1, :]
        k_s = kf_ref[h, r0 + s:r0 + s + 1, :]
        if s < half:
          d = b_lo - b_s
          if s > 0:
            d = d + jnp.where(rows8 >= s, 0.0, NEG_BIG)
          col = jnp.sum(q_lo * k_s * jnp.exp2(d), axis=-1, keepdims=True)
          band_lo = jnp.where(lane64 == r0 + s, col, band_lo)
          d = b_hi - b_s
        else:
          d = b_hi - b_s
          if s > half:
            d = d + jnp.where(rows8 >= s - half, 0.0, NEG_BIG)
        col = jnp.sum(q_hi * k_s * jnp.exp2(d), axis=-1, keepdims=True)
        band_hi = jnp.where(lane64 == r0 + s, col, band_hi)
      bands[h][i] = jnp.concatenate([band_lo, band_hi], axis=0)
  for i in range(1, nsub):
    r0 = i * SUB
    for h in heads:
      ref = b2[h][r0:r0 + 1]
      qh = q[h][r0:r0 + SUB] * jnp.exp2(b2[h][r0:r0 + SUB] - ref)
      kh = jnp.concatenate([k[h][:r0] * jnp.exp2(ref - b2[h][:r0]),
                            jnp.zeros((CHUNK - r0, GLA_DK), F32)], axis=0)
      bands[h][i] = bands[h][i] + _mm_nt(qh, kh)
  o = [o_inter[h] + _mm(jnp.concatenate(bands[h], axis=0), v[h]) for h in heads]
  for h in heads:
    st_ref[h] = (st_ref[h] * jnp.exp2(b2_last[h])
                 + _mm_tn(v[h], k[h] * jnp.exp2(b2_last[h] - b2[h])))
  for h in heads:
    gate = pv[rows, lanes(GLA_G0, vs[h])].astype(F32)
    o_ref[0, rows, vs[h]] = _gated_rmsnorm(o[h], gate, nw).astype(BF16)


def _gla_mixer_kernel(x_ref, w_ref, wup_ref, bgk_ref, nw_ref, o_ref,
                      proj_ref, b_ref, st_ref, b2_ref, kf_ref):
  tm = GLA_ROWS
  n_chunks = tm // CHUNK
  piece = (GLA_QKV + GLA_VAL) // n_chunks

  @pl.when(pl.program_id(1) == 0)
  def _():
    st_ref[...] = jnp.zeros(st_ref.shape, F32)
    proj_ref[...] = jnp.zeros(proj_ref.shape, BF16)
    b_ref[...] = jnp.zeros(b_ref.shape, F32)

  cur = lax.rem(pl.program_id(1), 2)
  prev = 1 - cur
  pv_prev, bv_prev = proj_ref.at[prev], b_ref.at[prev]
  pv_cur, bv_cur = proj_ref.at[cur], b_ref.at[cur]
  nw = nw_ref[...]
  xb = x_ref[0]

  low = jnp.dot(xb, w_ref[0, :, GLA_QKV + GLA_VAL:], preferred_element_type=F32)
  lmat = _chunk_tri(CHUNK, lower=True)
  for c in range(n_chunks):
    rows = slice(c * CHUNK, (c + 1) * CHUNK)
    cols = slice(c * piece, (c + 1) * piece)
    acc = jnp.dot(xb, w_ref[0, :, cols], preferred_element_type=F32)
    logit = _mm(low[rows], wup_ref[...]) + bgk_ref[...]
    gk = (jnp.minimum(logit, 0.0) - jnp.log1p(jnp.exp(-jnp.abs(logit)))) * (1.0 / GLA_GATE_NORM)
    bsum = jnp.zeros((CHUNK, GLA_KEY), F32)
    for part in _split3(gk):
      bsum = bsum + jnp.dot(lmat, part, preferred_element_type=F32)
    _gla_chunk_all_heads(c, pv_prev, bv_prev, nw, o_ref, st_ref, b2_ref, kf_ref)
    n_q = max(0, min(piece, GLA_KEY - c * piece))
    if n_q == piece:
      acc = acc * (GLA_DK ** -0.5)
    elif n_q:
      acc = jnp.concatenate([acc[:, :n_q] * (GLA_DK ** -0.5), acc[:, n_q:]], axis=1)
    pv_cur[:, cols] = acc.astype(BF16)
    bv_cur[rows, :] = bsum


def _gla_mixer_call(xb, w, layer, wup, bgk, norm_w):
  B, T, D = xb.shape
  tm = GLA_ROWS
  n_t = T // tm
  const = lambda b, t: (0, 0)
  return pl.pallas_call(
      _gla_mixer_kernel,
      grid=(B, n_t + 1),
      in_specs=[
          pl.BlockSpec((1, tm, D), lambda b, t: (b, jnp.minimum(t, n_t - 1), 0)),
          pl.BlockSpec((1,) + w.shape[1:], lambda b, t: (layer, 0, 0)),
          pl.BlockSpec((GLA_RANK, GLA_KEY), const),
          pl.BlockSpec((1, GLA_KEY), const),
          pl.BlockSpec((1, GLA_DV), const),
      ],
      out_specs=pl.BlockSpec((1, tm, GLA_VAL), lambda b, t: (b, jnp.maximum(t - 1, 0), 0)),
      out_shape=jax.ShapeDtypeStruct((B, T, GLA_VAL), BF16),
      scratch_shapes=[
          pltpu.VMEM((2, tm, GLA_QKV + GLA_VAL), BF16),
          pltpu.VMEM((2, tm, GLA_KEY), F32),
          pltpu.VMEM((GLA_HEADS, GLA_DV, GLA_DK), F32),
          pltpu.VMEM((GLA_HEADS, CHUNK, GLA_DK), F32),
          pltpu.VMEM((GLA_HEADS, CHUNK, GLA_DK), F32),
      ],
      compiler_params=pltpu.CompilerParams(
          dimension_semantics=("arbitrary", "arbitrary"),
          vmem_limit_bytes=VMEM_LIMIT_BYTES),
      name="gla_mixer",
  )(xb, w, wup, bgk, norm_w.reshape(1, GLA_DV))


def _out_kernel(o_ref, x_ref, w_ref, g_ref, b_ref, y_ref, *maybe_yb_ref):
  nblk = OUT_ROWS // OUT_BLOCK
  rows_of = lambda i: slice(i * OUT_BLOCK, (i + 1) * OUT_BLOCK)
  proj = lambda i: jnp.dot(o_ref[rows_of(i), :], w_ref[0], preferred_element_type=F32)
  y = proj(0)
  for i in range(nblk):
    y_next = proj(i + 1) if i + 1 < nblk else None
    z = DEEP_ALPHA * x_ref[rows_of(i), :] + y
    mu = jnp.mean(z, axis=-1, keepdims=True)
    zc = z - mu
    var = jnp.mean(zc * zc, axis=-1, keepdims=True)
    out = zc * lax.rsqrt(var + LN_EPS) * g_ref[...] + b_ref[...]
    y_ref[rows_of(i), :] = out
    for yb_ref in maybe_yb_ref:
      yb_ref[rows_of(i), :] = out.astype(BF16)
    y = y_next


def _out_proj_norm(o, x, w_out, layer, ln_g, ln_b, with_bf16_copy):
  n, dv = o.shape
  tm = OUT_ROWS
  const = lambda i: (0, 0)
  rowmap = lambda i: (i, 0)
  n_out = 2 if with_bf16_copy else 1
  outs = pl.pallas_call(
      _out_kernel,
      grid=(n // tm,),
      in_specs=[
          pl.BlockSpec((tm, dv), rowmap),
          pl.BlockSpec((tm, D_MODEL), rowmap),
          pl.BlockSpec((1, dv, D_MODEL), lambda i: (layer, 0, 0)),
          pl.BlockSpec((1, D_MODEL), const),
          pl.BlockSpec((1, D_MODEL), const),
      ],
      out_specs=[pl.BlockSpec((tm, D_MODEL), rowmap)] * n_out,
      out_shape=[jax.ShapeDtypeStruct((n, D_MODEL), F32),
                 jax.ShapeDtypeStruct((n, D_MODEL), BF16)][:n_out],
      compiler_params=pltpu.CompilerParams(
          dimension_semantics=("arbitrary",),
          vmem_limit_bytes=VMEM_LIMIT_BYTES),
      name="out_proj_norm",
  )(o, x, w_out, ln_g.reshape(1, D_MODEL), ln_b.reshape(1, D_MODEL))
  return outs[0], (outs[1] if with_bf16_copy else None)


def _cast_kernel(w_ref, o_ref):
  o_ref[...] = w_ref[...].astype(BF16)


def _to_bf16(w):
  n_layers, rows, cols = w.shape
  spec = pl.BlockSpec((1, CAST_ROWS, cols), lambda l, r: (l, r, 0))
  return pl.pallas_call(
      _cast_kernel,
      grid=(n_layers, rows // CAST_ROWS),
      in_specs=[spec],
      out_specs=spec,
      out_shape=jax.ShapeDtypeStruct(w.shape, BF16),
      compiler_params=pltpu.CompilerParams(
          dimension_semantics=("arbitrary", "arbitrary"),
          vmem_limit_bytes=VMEM_LIMIT_BYTES),
      name="weights_to_bf16",
  )(w)


def _gdn_mixer(x_in, w_all, layer, w_f32, conv_w, a_log, dt_bias, norm_w):
  H = GDN_HEADS
  order = jnp.array(list(range(0, H, 2)) + list(range(1, H, 2)), jnp.int32)
  wat = w_f32[:, GDN_QKV + GDN_VAL + H:].T[order].astype(BF16)
  q, k, v, gate, beta, gcc, gcr = _gdn_in_proj(
      x_in, w_all, layer, wat, conv_w, a_log.reshape(1, H), dt_bias.reshape(1, H),
      a_log[order].reshape(H, 1), dt_bias[order].reshape(H, 1))
  return _gdn_recurrence(q, k, v, gate, beta, gcc, gcr, norm_w)


def kernel(x, gdn_w_in, gdn_conv_w, gdn_a_log, gdn_dt_bias, gdn_norm_w, gdn_w_out,
           gla_w_in, gla_w_gk_up, gla_b_gk, gla_norm_w, gla_w_out, ln_g, ln_b):
  B, T, D = x.shape
  gdn_w_in_b, gdn_w_out_b = _to_bf16(gdn_w_in), _to_bf16(gdn_w_out)
  gla_w_in_b, gla_w_out_b = _to_bf16(gla_w_in), _to_bf16(gla_w_out)
  x_in = x
  x = x.reshape(B * T, D)
  for i in range(DEPTH):
    j = i // 2
    if i % 2 == 0:
      o = _gdn_mixer(x_in, gdn_w_in_b, j, gdn_w_in[j], gdn_conv_w[j], gdn_a_log[j],
                     gdn_dt_bias[j], gdn_norm_w[j])
      w_out = gdn_w_out_b
    else:
      o = _gla_mixer_call(x_in, gla_w_in_b, j, gla_w_gk_up[j].astype(BF16),
                          gla_b_gk[j].reshape(1, GLA_KEY), gla_norm_w[j])
      w_out = gla_w_out_b
    x, xb = _out_proj_norm(o.reshape(B * T, -1), x, w_out, j, ln_g[i], ln_b[i],
                           with_bf16_copy=i + 1 < DEPTH)
    if xb is not None:
      x_in = xb.reshape(B, T, D)
  return x.reshape(B, T, D)
```

```python
import jax
import jax.numpy as jnp
from jax import lax
from jax.experimental import pallas as pl
from jax.experimental.pallas import tpu as pltpu

F32 = jnp.float32
BF16 = jnp.bfloat16

D_MODEL = 1024
DEPTH = 4
CHUNK = 64
CHUNK_SHIFT = 6
CONV_K = 4

GDN_HEADS = 8
GDN_DK = 128
GDN_DV = 256
GDN_KEY = GDN_HEADS * GDN_DK
GDN_VAL = GDN_HEADS * GDN_DV
GDN_QKV = 2 * GDN_KEY + GDN_VAL
GDN_PAIRS = GDN_HEADS // 2

GLA_HEADS = 4
GLA_DK = 128
GLA_DV = 256
GLA_KEY = GLA_HEADS * GLA_DK
GLA_VAL = GLA_HEADS * GLA_DV
GLA_QKV = 2 * GLA_KEY + GLA_VAL
GLA_RANK = 16
GLA_GATE_NORM = 16.0

DEEP_ALPHA = (2.0 * DEPTH) ** 0.25
LN_EPS = 1e-5
RMS_EPS = 1e-6
L2_EPS = 1e-6
LOG2E = 1.4426950408889634

V7X_MXU_WIDTH = 256
V7X_SUBLANES = 8
V7X_LANES = 128
VMEM_LIMIT_BYTES = 56 * 1024 * 1024

IN_ROWS = 512
REC_ROWS = 512
OUT_ROWS = 1024
OUT_BLOCK = 256
COL_TILE = V7X_MXU_WIDTH
SUB = 16
NEG_BIG = -1e30


def _mm(a, b):
  return jnp.dot(a.astype(BF16), b.astype(BF16), preferred_element_type=F32)


def _mm_nt(a, b):
  return lax.dot_general(a.astype(BF16), b.astype(BF16),
                         (((1,), (1,)), ((), ())), preferred_element_type=F32)


def _mm_tn(a, b):
  return lax.dot_general(a.astype(BF16), b.astype(BF16),
                         (((0,), (0,)), ((), ())), preferred_element_type=F32)


def _split3(x):
  hi = x.astype(BF16)
  r1 = x - hi.astype(F32)
  mid = r1.astype(BF16)
  lo = (r1 - mid.astype(F32)).astype(BF16)
  return hi, mid, lo


def _sigmoid(x):
  return 1.0 / (1.0 + jnp.exp(-x))


def _softplus(x):
  return jnp.maximum(x, 0.0) + jnp.log1p(jnp.exp(-jnp.abs(x)))


def _silu(x):
  return x * _sigmoid(x)


def _chunk_tri(n, lower):
  r = lax.broadcasted_iota(jnp.int32, (n, n), 0)
  c = lax.broadcasted_iota(jnp.int32, (n, n), 1)
  same = jnp.right_shift(r, CHUNK_SHIFT) == jnp.right_shift(c, CHUNK_SHIFT)
  tri = (c <= r) if lower else (r <= c)
  return jnp.where(same & tri, 1.0, 0.0).astype(BF16)


def _gated_rmsnorm(o, gate, nw):
  ms = jnp.mean(o * o, axis=-1, keepdims=True)
  return o * lax.rsqrt(ms + RMS_EPS) * nw * _silu(gate)


def _gdn_in_kernel(x_ref, w_ref, wat_ref, convw_ref, alog_ref, dtb_ref, alogt_ref, dtbt_ref,
                   q_ref, k_ref, v_ref, gate_ref, beta_ref, gcc_ref, gcr_ref, pbuf):
  tm = IN_ROWS
  halo = V7X_SUBLANES
  H = GDN_HEADS

  @pl.when(pl.program_id(1) == 0)
  def _():
    pbuf[0:halo, :] = jnp.zeros((halo, GDN_QKV), F32)

  xb = x_ref[0].astype(BF16)

  def gate_cols(c):
    cs = slice(c * COL_TILE, (c + 1) * COL_TILE)
    ws = slice(GDN_QKV + c * COL_TILE, GDN_QKV + (c + 1) * COL_TILE)
    gate_ref[0, :, cs] = jnp.dot(xb, w_ref[0, :, ws], preferred_element_type=F32).astype(BF16)

  n_cols = GDN_QKV // COL_TILE
  proj = lambda c: jnp.dot(xb, w_ref[0, :, c * COL_TILE:(c + 1) * COL_TILE],
                           preferred_element_type=F32)
  acc_next = proj(0)
  for c in range(n_cols):
    cs = slice(c * COL_TILE, (c + 1) * COL_TILE)
    acc = acc_next
    pbuf[halo:halo + tm, cs] = acc
    if c + 1 < n_cols:
      acc_next = proj(c + 1)
    cw = convw_ref[:, cs]
    y = acc * cw[CONV_K - 1:CONV_K]
    for j in range(CONV_K - 1):
      off = halo - (CONV_K - 1) + j
      y = y + pbuf[off:off + tm, cs] * cw[j:j + 1]
    y = _silu(y)
    col0 = c * COL_TILE
    if col0 < 2 * GDN_KEY:
      parts = []
      for j in range(COL_TILE // GDN_DK):
        t = y[:, j * GDN_DK:(j + 1) * GDN_DK]
        parts.append(t * lax.rsqrt(jnp.sum(t * t, axis=-1, keepdims=True) + L2_EPS))
      y = jnp.concatenate(parts, axis=1)
      if col0 < GDN_KEY:
        q_ref[0, :, cs] = (y * (GDN_DK ** -0.5)).astype(BF16)
      else:
        k_ref[0, :, col0 - GDN_KEY:col0 - GDN_KEY + COL_TILE] = y.astype(BF16)
    else:
      v_ref[0, :, col0 - 2 * GDN_KEY:col0 - 2 * GDN_KEY + COL_TILE] = y.astype(BF16)
    if c % 2 == 1:
      gate_cols(c // 2)
  pbuf[0:halo, :] = pbuf[tm:tm + halo, :]

  ba = jnp.dot(xb, w_ref[0, :, GDN_QKV + GDN_VAL:], preferred_element_type=F32)
  beta_ref[0] = _sigmoid(ba[:, :H])
  g = -jnp.exp(alog_ref[...]) * _softplus(ba[:, H:] + dtb_ref[...])
  lmat = _chunk_tri(tm, lower=True)
  gc = jnp.zeros((tm, H), F32)
  for part in _split3(g):
    gc = gc + jnp.dot(lmat, part, preferred_element_type=F32)
  gcc_ref[0] = gc

  at = lax.dot_general(wat_ref[...], xb, (((1,), (1,)), ((), ())),
                       preferred_element_type=F32)
  gt = -jnp.exp(alogt_ref[...]) * _softplus(at + dtbt_ref[...])
  umat = _chunk_tri(tm, lower=False)
  gct = jnp.zeros((H, tm), F32)
  for part in _split3(gt):
    gct = gct + jnp.dot(part, umat, preferred_element_type=F32)
  even = gct[:GDN_PAIRS]
  odd = gct[GDN_PAIRS:]
  odd_r = pltpu.roll(gct, CHUNK, 1)[GDN_PAIRS:]
  even_r = pltpu.roll(gct, tm - CHUNK, 1)[:GDN_PAIRS]
  low_half = (lax.broadcasted_iota(jnp.int32, (GDN_PAIRS, tm), 1) & (V7X_LANES - 1)) < CHUNK
  p_even = jnp.where(low_half, even, odd_r)
  p_odd = jnp.where(low_half, even_r, odd)
  for j in range(tm // V7X_LANES):
    ls = slice(j * V7X_LANES, (j + 1) * V7X_LANES)
    gcr_ref[0, 2 * j] = p_even[:, ls]
    gcr_ref[0, 2 * j + 1] = p_odd[:, ls]


def _gdn_in_proj(xb, w, layer, wat, convw, alog, dtb, alogt, dtbt):
  B, T, D = xb.shape
  tm = IN_ROWS
  H = GDN_HEADS
  const = lambda b, t: (0, 0)
  rowmap = lambda b, t: (b, t, 0)
  return pl.pallas_call(
      _gdn_in_kernel,
      grid=(B, T // tm),
      in_specs=[
          pl.BlockSpec((1, tm, D), rowmap),
          pl.BlockSpec((1,) + w.shape[1:], lambda b, t: (layer, 0, 0)),
          pl.BlockSpec((H, D), const),
          pl.BlockSpec((CONV_K, GDN_QKV), const),
          pl.BlockSpec((1, H), const),
          pl.BlockSpec((1, H), const),
          pl.BlockSpec((H, 1), const),
          pl.BlockSpec((H, 1), const),
      ],
      out_specs=[
          pl.BlockSpec((1, tm, GDN_KEY), rowmap),
          pl.BlockSpec((1, tm, GDN_KEY), rowmap),
          pl.BlockSpec((1, tm, GDN_VAL), rowmap),
          pl.BlockSpec((1, tm, GDN_VAL), rowmap),
          pl.BlockSpec((1, tm, H), rowmap),
          pl.BlockSpec((1, tm, H), rowmap),
          pl.BlockSpec((1, tm // CHUNK, GDN_PAIRS, V7X_LANES), lambda b, t: (b, t, 0, 0)),
      ],
      out_shape=[
          jax.ShapeDtypeStruct((B, T, GDN_KEY), BF16),
          jax.ShapeDtypeStruct((B, T, GDN_KEY), BF16),
          jax.ShapeDtypeStruct((B, T, GDN_VAL), BF16),
          jax.ShapeDtypeStruct((B, T, GDN_VAL), BF16),
          jax.ShapeDtypeStruct((B, T, H), F32),
          jax.ShapeDtypeStruct((B, T, H), F32),
          jax.ShapeDtypeStruct((B, T // CHUNK, GDN_PAIRS, V7X_LANES), F32),
      ],
      scratch_shapes=[pltpu.VMEM((tm + V7X_SUBLANES, GDN_QKV), F32)],
      compiler_params=pltpu.CompilerParams(
          dimension_semantics=("arbitrary", "arbitrary"),
          vmem_limit_bytes=VMEM_LIMIT_BYTES),
      name="gdn_in_proj",
  )(xb, w, wat, convw, alog, dtb, alogt, dtbt)


def _block_diag(y, bdmask):
  n = y.shape[1] // CHUNK
  return jnp.where(bdmask, jnp.concatenate([y] * n, axis=0), jnp.zeros((), BF16))


def _packed_inverse_minus_eye(a_list, row, colp, bdmask, out):
  same16 = jnp.right_shift(row, 4) == jnp.right_shift(colp, 4)
  same32 = jnp.right_shift(row, 5) == jnp.right_shift(colp, 5)
  bd = lambda xs: [_block_diag(x.astype(BF16), bdmask) for x in xs]
  mm = lambda xs, ybs: [jnp.dot(x.astype(BF16), yb, preferred_element_type=F32)
                        for x, yb in zip(xs, ybs)]
  stack = lambda xs, ys: [jnp.concatenate([x, y], axis=0) for x, y in zip(xs, ys)]
  n = [jnp.where(same16, -a, 0.0) for a in a_list]
  s = mm(n, bd(n))
  yield
  p = n
  for _ in range(2):
    both = mm(stack(p, s), bd(s))
    yield
    p = [x + y + z[:CHUNK] for x, y, z in zip(p, s, both)]
    s = [z[CHUNK:] for z in both]
  ps = mm(p, bd(s))
  yield
  p = [x + y + z for x, y, z in zip(p, s, ps)]
  for mask in (same32 & jnp.logical_not(same16), jnp.logical_not(same32)):
    e = [jnp.where(mask, a, 0.0) for a in a_list]
    pe = mm(p, bd(e))
    yield
    f = [x + y for x, y in zip(e, pe)]
    fp = mm(f, bd(p))
    yield
    p = [x - y - z for x, y, z in zip(p, f, fp)]
  out.extend(p)


def _split_rows(x, low_half):
  return jnp.concatenate([jnp.where(low_half, x, 0.0), jnp.where(low_half, 0.0, x)], axis=0)


def _interleave(*gens):
  gens = list(gens)
  while gens:
    for g in list(gens):
      try:
        next(g)
      except StopIteration:
        gens.remove(g)


def _gdn_rec_kernel(q_ref, k_ref, v_ref, gate_ref, beta_ref, gcc_ref, gcr_ref,
                    nw_ref, o_ref, s_ref, uw_ref, qk_ref):
  @pl.when(pl.program_id(1) == 0)
  def _():
    s_ref[...] = jnp.zeros(s_ref.shape, F32)

  n_chunks = REC_ROWS // CHUNK
  heads = range(GDN_HEADS)
  nw = nw_ref[...]
  row = lax.broadcasted_iota(jnp.int32, (CHUNK, V7X_LANES), 0)
  lane = lax.broadcasted_iota(jnp.int32, (CHUNK, V7X_LANES), 1)
  colp = lane & (CHUNK - 1)
  low_half = lane < CHUNK
  incl = row >= colp
  strict = row > colp
  bd_r = lax.broadcasted_iota(jnp.int32, (2 * CHUNK, V7X_LANES), 0)
  bd_l = lax.broadcasted_iota(jnp.int32, (2 * CHUNK, V7X_LANES), 1)
  bdmask = (bd_r < CHUNK) == (bd_l < CHUNK)
  kbd_r = lax.broadcasted_iota(jnp.int32, (2 * CHUNK, 2 * GDN_DK), 0)
  kbd_l = lax.broadcasted_iota(jnp.int32, (2 * CHUNK, 2 * GDN_DK), 1)
  kbdmask = (kbd_r < CHUNK) == (kbd_l < GDN_DK)
  first_head_k = lax.broadcasted_iota(jnp.int32, (CHUNK, 2 * GDN_DK), 1) < GDN_DK

  rows_of = lambda c: slice(c * CHUNK, (c + 1) * CHUNK)
  klanes = lambda h: slice(h * GDN_DK, (h + 1) * GDN_DK)
  vlanes = lambda h: slice(h * GDN_DV, (h + 1) * GDN_DV)
  col_of = lambda ref, c, h: ref[0, rows_of(c), h:h + 1]

  def phase_a(chunks):
    probs = [(c, p) for c in chunks for p in range(GDN_PAIRS)]
    kb_list, a_list = [], []
    for c, p in probs:
      h0, h1 = 2 * p, 2 * p + 1
      pl2 = slice(h0 * GDN_DK, (h1 + 1) * GDN_DK)
      kp = k_ref[0, rows_of(c), pl2]
      beta_k = jnp.where(first_head_k, col_of(beta_ref, c, h0), col_of(beta_ref, c, h1))
      kb = kp.astype(F32) * beta_k
      lhs = jnp.concatenate([kb.astype(BF16), q_ref[0, rows_of(c), pl2]], axis=0)
      kbd = jnp.where(kbdmask, jnp.concatenate([kp, kp], axis=0), jnp.zeros((), BF16))
      aq = lax.dot_general(lhs, kbd, (((1,), (1,)), ((), ())), preferred_element_type=F32)
      gccp = jnp.where(low_half, col_of(gcc_ref, c, h0), col_of(gcc_ref, c, h1))
      gcrp = gcr_ref[0, c, p:p + 1, :]
      decay = jnp.where(incl, jnp.exp(jnp.where(incl, gccp - gcrp, 0.0)), 0.0)
      a_list.append(jnp.where(strict, aq[:CHUNK] * decay, 0.0))
      qk_ref[c, p] = aq[CHUNK:] * decay
      kb_list.append(kb)
    yield
    c_list = []
    yield from _packed_inverse_minus_eye(a_list, row, colp, bdmask, c_list)
    for (c, p), kb, cm in zip(probs, kb_list, c_list):
      rhs = []
      for j, h in enumerate((2 * p, 2 * p + 1)):
        beta_h = col_of(beta_ref, c, h)
        eg_h = jnp.exp(col_of(gcc_ref, c, h))
        rhs.append(jnp.concatenate(
            [v_ref[0, rows_of(c), vlanes(h)].astype(F32) * beta_h,
             kb[:, j * GDN_DK:(j + 1) * GDN_DK] * eg_h], axis=1))
      rhs = jnp.concatenate(rhs, axis=0)
      uw = rhs + _mm(_split_rows(cm, low_half), rhs)
      uw_ref[c, 2 * p] = uw[:CHUNK]
      uw_ref[c, 2 * p + 1] = uw[CHUNK:]
    yield

  def phase_b(chunks):
    for c in chunks:
      gcc = [col_of(gcc_ref, c, h) for h in heads]
      g_last = [x[CHUNK - 1:CHUNK, :] for x in gcc]
      r = []
      for h in heads:
        qd = q_ref[0, rows_of(c), klanes(h)].astype(F32) * jnp.exp(gcc[h])
        r.append(_mm(jnp.concatenate([uw_ref[c, h, :, GDN_DV:], qd], axis=0), s_ref[h]))
      yield
      v_new = [uw_ref[c, h, :, :GDN_DV] - r[h][:CHUNK] for h in heads]
      o = [None] * GDN_HEADS
      for p in range(GDN_PAIRS):
        h0, h1 = 2 * p, 2 * p + 1
        res = _mm(_split_rows(qk_ref[c, p], low_half),
                  jnp.concatenate([v_new[h0], v_new[h1]], axis=0))
        o[h0] = r[h0][CHUNK:] + res[:CHUNK]
        o[h1] = r[h1][CHUNK:] + res[CHUNK:]
      for h in heads:
        k_dec = k_ref[0, rows_of(c), klanes(h)].astype(F32) * jnp.exp(g_last[h] - gcc[h])
        s_ref[h] = s_ref[h] * jnp.exp(g_last[h]) + _mm_tn(k_dec, v_new[h])
      for h in heads:
        gate = gate_ref[0, rows_of(c), vlanes(h)].astype(F32)
        o_ref[0, rows_of(c), vlanes(h)] = _gated_rmsnorm(o[h], gate, nw).astype(BF16)
      yield

  first, second = range(n_chunks // 2), range(n_chunks // 2, n_chunks)
  _interleave(phase_a(first))
  _interleave(phase_a(second), phase_b(first))
  _interleave(phase_b(second))


def _gdn_recurrence(q, k, v, gate, beta, gcc, gcr, norm_w):
  B, T, _ = q.shape
  tt = REC_ROWS
  H = GDN_HEADS
  rowmap = lambda b, t: (b, t, 0)
  return pl.pallas_call(
      _gdn_rec_kernel,
      grid=(B, T // tt),
      in_specs=[
          pl.BlockSpec((1, tt, GDN_KEY), rowmap),
          pl.BlockSpec((1, tt, GDN_KEY), rowmap),
          pl.BlockSpec((1, tt, GDN_VAL), rowmap),
          pl.BlockSpec((1, tt, GDN_VAL), rowmap),
          pl.BlockSpec((1, tt, H), rowmap),
          pl.BlockSpec((1, tt, H), rowmap),
          pl.BlockSpec((1, tt // CHUNK, GDN_PAIRS, V7X_LANES), lambda b, t: (b, t, 0, 0)),
          pl.BlockSpec((1, GDN_DV), lambda b, t: (0, 0)),
      ],
      out_specs=pl.BlockSpec((1, tt, GDN_VAL), rowmap),
      out_shape=jax.ShapeDtypeStruct((B, T, GDN_VAL), BF16),
      scratch_shapes=[
          pltpu.VMEM((H, GDN_DK, GDN_DV), F32),
          pltpu.VMEM((tt // CHUNK, H, CHUNK, GDN_DV + GDN_DK), F32),
          pltpu.VMEM((tt // CHUNK, GDN_PAIRS, CHUNK, V7X_LANES), F32),
      ],
      compiler_params=pltpu.CompilerParams(
          dimension_semantics=("arbitrary", "arbitrary"),
          vmem_limit_bytes=VMEM_LIMIT_BYTES),
      name="gdn_recurrence",
  )(q, k, v, gate, beta, gcc, gcr, norm_w.reshape(1, GDN_DV))


GLA_ROWS = 256
GLA_Q0, GLA_K0, GLA_V0, GLA_G0 = 0, GLA_KEY, 2 * GLA_KEY, GLA_QKV


def _gla_chunk_all_heads(c, pv, bv, nw, o_ref, st_ref, b2_ref, kf_ref):
  heads = range(GLA_HEADS)
  nsub = CHUNK // SUB
  half = V7X_SUBLANES
  rows = slice(c * CHUNK, (c + 1) * CHUNK)
  ks = [slice(h * GLA_DK, (h + 1) * GLA_DK) for h in heads]
  vs = [slice(h * GLA_DV, (h + 1) * GLA_DV) for h in heads]
  lanes = lambda base, sl: slice(base + sl.start, base + sl.stop)
  q = [pv[rows, lanes(GLA_Q0, ks[h])].astype(F32) for h in heads]
  k = [pv[rows, lanes(GLA_K0, ks[h])].astype(F32) for h in heads]
  v = [pv[rows, lanes(GLA_V0, vs[h])] for h in heads]
  b2 = [bv[rows, ks[h]] * LOG2E for h in heads]
  b2_last = [x[CHUNK - 1:CHUNK, :] for x in b2]
  for h in heads:
    b2_ref[h] = b2[h]
    kf_ref[h] = k[h]
  o_inter = [_mm_nt(q[h] * jnp.exp2(b2[h]), st_ref[h]) for h in heads]

  rows8 = lax.broadcasted_iota(jnp.int32, (half, 1), 0)
  lane64 = lax.broadcasted_iota(jnp.int32, (half, CHUNK), 1)
  bands = [[None] * nsub for _ in heads]
  for h in heads:
    for i in range(nsub):
      r0 = i * SUB
      q_lo, q_hi = q[h][r0:r0 + half], q[h][r0 + half:r0 + SUB]
      b_lo, b_hi = b2[h][r0:r0 + half], b2[h][r0 + half:r0 + SUB]
      band_lo = jnp.zeros((half, CHUNK), F32)
      band_hi = jnp.zeros((half, CHUNK), F32)
      for s in range(SUB):
        b_s = b2_ref[h, r0 + s:r0 + s + 1, :]
        k_s = kf_ref[h, r0 + s:r0 + s + 1, :]
        if s < half:
          d = b_lo - b_s
          if s > 0:
            d = d + jnp.where(rows8 >= s, 0.0, NEG_BIG)
          col = jnp.sum(q_lo * k_s * jnp.exp2(d), axis=-1, keepdims=True)
          band_lo = jnp.where(lane64 == r0 + s, col, band_lo)
          d = b_hi - b_s
        else:
          d = b_hi - b_s
          if s > half:
            d = d + jnp.where(rows8 >= s - half, 0.0, NEG_BIG)
        col = jnp.sum(q_hi * k_s * jnp.exp2(d), axis=-1, keepdims=True)
        band_hi = jnp.where(lane64 == r0 + s, col, band_hi)
      bands[h][i] = jnp.concatenate([band_lo, band_hi], axis=0)
  for i in range(1, nsub):
    r0 = i * SUB
    for h in heads:
      ref = b2[h][r0:r0 + 1]
      qh = q[h][r0:r0 + SUB] * jnp.exp2(b2[h][r0:r0 + SUB] - ref)
      kh = jnp.concatenate([k[h][:r0] * jnp.exp2(ref - b2[h][:r0]),
                            jnp.zeros((CHUNK - r0, GLA_DK), F32)], axis=0)
      bands[h][i] = bands[h][i] + _mm_nt(qh, kh)
  o = [o_inter[h] + _mm(jnp.concatenate(bands[h], axis=0), v[h]) for h in heads]
  for h in heads:
    st_ref[h] = (st_ref[h] * jnp.exp2(b2_last[h])
                 + _mm_tn(v[h], k[h] * jnp.exp2(b2_last[h] - b2[h])))
  for h in heads:
    gate = pv[rows, lanes(GLA_G0, vs[h])].astype(F32)
    o_ref[0, rows, vs[h]] = _gated_rmsnorm(o[h], gate, nw).astype(BF16)


def _gla_mixer_kernel(x_ref, w_ref, wup_ref, bgk_ref, nw_ref, o_ref,
                      proj_ref, b_ref, st_ref, b2_ref, kf_ref):
  tm = GLA_ROWS
  n_chunks = tm // CHUNK
  piece = (GLA_QKV + GLA_VAL) // n_chunks

  @pl.when(pl.program_id(1) == 0)
  def _():
    st_ref[...] = jnp.zeros(st_ref.shape, F32)
    proj_ref[...] = jnp.zeros(proj_ref.shape, BF16)
    b_ref[...] = jnp.zeros(b_ref.shape, F32)

  cur = lax.rem(pl.program_id(1), 2)
  prev = 1 - cur
  pv_prev, bv_prev = proj_ref.at[prev], b_ref.at[prev]
  pv_cur, bv_cur = proj_ref.at[cur], b_ref.at[cur]
  nw = nw_ref[...]
  xb = x_ref[0]

  low = jnp.dot(xb, w_ref[0, :, GLA_QKV + GLA_VAL:], preferred_element_type=F32)
  lmat = _chunk_tri(CHUNK, lower=True)
  for c in range(n_chunks):
    rows = slice(c * CHUNK, (c + 1) * CHUNK)
    cols = slice(c * piece, (c + 1) * piece)
    acc = jnp.dot(xb, w_ref[0, :, cols], preferred_element_type=F32)
    logit = _mm(low[rows], wup_ref[...]) + bgk_ref[...]
    gk = (jnp.minimum(logit, 0.0) - jnp.log1p(jnp.exp(-jnp.abs(logit)))) * (1.0 / GLA_GATE_NORM)
    bsum = jnp.zeros((CHUNK, GLA_KEY), F32)
    for part in _split3(gk):
      bsum = bsum + jnp.dot(lmat, part, preferred_element_type=F32)
    _gla_chunk_all_heads(c, pv_prev, bv_prev, nw, o_ref, st_ref, b2_ref, kf_ref)
    n_q = max(0, min(piece, GLA_KEY - c * piece))
    if n_q == piece:
      acc = acc * (GLA_DK ** -0.5)
    elif n_q:
      acc = jnp.concatenate([acc[:, :n_q] * (GLA_DK ** -0.5), acc[:, n_q:]], axis=1)
    pv_cur[:, cols] = acc.astype(BF16)
    bv_cur[rows, :] = bsum


def _gla_mixer_call(xb, w, layer, wup, bgk, norm_w):
  B, T, D = xb.shape
  tm = GLA_ROWS
  n_t = T // tm
  const = lambda b, t: (0, 0)
  return pl.pallas_call(
      _gla_mixer_kernel,
      grid=(B, n_t + 1),
      in_specs=[
          pl.BlockSpec((1, tm, D), lambda b, t: (b, jnp.minimum(t, n_t - 1), 0)),
          pl.BlockSpec((1,) + w.shape[1:], lambda b, t: (layer, 0, 0)),
          pl.BlockSpec((GLA_RANK, GLA_KEY), const),
          pl.BlockSpec((1, GLA_KEY), const),
          pl.BlockSpec((1, GLA_DV), const),
      ],
      out_specs=pl.BlockSpec((1, tm, GLA_VAL), lambda b, t: (b, jnp.maximum(t - 1, 0), 0)),
      out_shape=jax.ShapeDtypeStruct((B, T, GLA_VAL), BF16),
      scratch_shapes=[
          pltpu.VMEM((2, tm, GLA_QKV + GLA_VAL), BF16),
          pltpu.VMEM((2, tm, GLA_KEY), F32),
          pltpu.VMEM((GLA_HEADS, GLA_DV, GLA_DK), F32),
          pltpu.VMEM((GLA_HEADS, CHUNK, GLA_DK), F32),
          pltpu.VMEM((GLA_HEADS, CHUNK, GLA_DK), F32),
      ],
      compiler_params=pltpu.CompilerParams(
          dimension_semantics=("arbitrary", "arbitrary"),
          vmem_limit_bytes=VMEM_LIMIT_BYTES),
      name="gla_mixer",
  )(xb, w, wup, bgk, norm_w.reshape(1, GLA_DV))


def _out_kernel(o_ref, x_ref, w_ref, g_ref, b_ref, y_ref, *maybe_yb_ref):
  nblk = OUT_ROWS // OUT_BLOCK
  rows_of = lambda i: slice(i * OUT_BLOCK, (i + 1) * OUT_BLOCK)
  proj = lambda i: jnp.dot(o_ref[rows_of(i), :], w_ref[0], preferred_element_type=F32)
  y = proj(0)
  for i in range(nblk):
    y_next = proj(i + 1) if i + 1 < nblk else None
    z = DEEP_ALPHA * x_ref[rows_of(i), :] + y
    mu = jnp.mean(z, axis=-1, keepdims=True)
    zc = z - mu
    var = jnp.mean(zc * zc, axis=-1, keepdims=True)
    out = zc * lax.rsqrt(var + LN_EPS) * g_ref[...] + b_ref[...]
    y_ref[rows_of(i), :] = out
    for yb_ref in maybe_yb_ref:
      yb_ref[rows_of(i), :] = out.astype(BF16)
    y = y_next


def _out_proj_norm(o, x, w_out, layer, ln_g, ln_b, with_bf16_copy):
  n, dv = o.shape
  tm = OUT_ROWS
  const = lambda i: (0, 0)
  rowmap = lambda i: (i, 0)
  n_out = 2 if with_bf16_copy else 1
  outs = pl.pallas_call(
      _out_kernel,
      grid=(n // tm,),
      in_specs=[
          pl.BlockSpec((tm, dv), rowmap),
          pl.BlockSpec((tm, D_MODEL), rowmap),
          pl.BlockSpec((1, dv, D_MODEL), lambda i: (layer, 0, 0)),
          pl.BlockSpec((1, D_MODEL), const),
          pl.BlockSpec((1, D_MODEL), const),
      ],
      out_specs=[pl.BlockSpec((tm, D_MODEL), rowmap)] * n_out,
      out_shape=[jax.ShapeDtypeStruct((n, D_MODEL), F32),
                 jax.ShapeDtypeStruct((n, D_MODEL), BF16)][:n_out],
      compiler_params=pltpu.CompilerParams(
          dimension_semantics=("arbitrary",),
          vmem_limit_bytes=VMEM_LIMIT_BYTES),
      name="out_proj_norm",
  )(o, x, w_out, ln_g.reshape(1, D_MODEL), ln_b.reshape(1, D_MODEL))
  return outs[0], (outs[1] if with_bf16_copy else None)


def _gdn_mixer(x_in, w_all, layer, w_a, conv_w, a_log, dt_bias, norm_w):
  H = GDN_HEADS
  order = jnp.array(list(range(0, H, 2)) + list(range(1, H, 2)), jnp.int32)
  wat = w_a.T[order].astype(BF16)
  q, k, v, gate, beta, gcc, gcr = _gdn_in_proj(
      x_in, w_all, layer, wat, conv_w, a_log.reshape(1, H), dt_bias.reshape(1, H),
      a_log[order].reshape(H, 1), dt_bias[order].reshape(H, 1))
  return _gdn_recurrence(q, k, v, gate, beta, gcc, gcr, norm_w)


def kernel(x, gdn_w_in, gdn_conv_w, gdn_a_log, gdn_dt_bias, gdn_norm_w, gdn_w_out,
           gla_w_in, gla_w_gk_up, gla_b_gk, gla_norm_w, gla_w_out, ln_g, ln_b):
  B, T, D = x.shape
  gdn_w_in_b, gdn_w_out_b = gdn_w_in.astype(BF16), gdn_w_out.astype(BF16)
  gla_w_in_b, gla_w_out_b = gla_w_in.astype(BF16), gla_w_out.astype(BF16)
  x_in = x
  x = x.reshape(B * T, D)
  for i in range(DEPTH):
    j = i // 2
    if i % 2 == 0:
      o = _gdn_mixer(x_in, gdn_w_in_b, j, gdn_w_in[j, :, GDN_QKV + GDN_VAL + GDN_HEADS:],
                     gdn_conv_w[j], gdn_a_log[j],
                     gdn_dt_bias[j], gdn_norm_w[j])
      w_out = gdn_w_out_b
    else:
      o = _gla_mixer_call(x_in, gla_w_in_b, j, gla_w_gk_up[j].astype(BF16),
                          gla_b_gk[j].reshape(1, GLA_KEY), gla_norm_w[j])
      w_out = gla_w_out_b
    x, xb = _out_proj_norm(o.reshape(B * T, -1), x, w_out, j, ln_g[i], ln_b[i],
                           with_bf16_copy=i + 1 < DEPTH)
    if xb is not None:
      x_in = xb.reshape(B, T, D)
  return x.reshape(B, T, D)
```
